```python
import jax, jax.numpy as jnp
from jax import lax
import numpy as np

D_MODEL = 1024
BATCH = 8
SEQ = 8192
DEPTH = 1
DEC_BATCH = 16
DEC_SEQ = 2048
PAST_LEN = 128

D_MIX = D_MODEL
HEAD_DIM = 64
D_ATTN = D_MIX // 2
N_HEADS = D_ATTN // HEAD_DIM
N_KV_HEADS = 2
Q_PER_KV = N_HEADS // N_KV_HEADS
D_KV = N_KV_HEADS * HEAD_DIM
D_CONV = D_MIX - D_ATTN
CONV_WIDTH = 31
CONV_PAD = CONV_WIDTH // 2
D_IN = D_ATTN + 2 * D_KV + 2 * D_CONV
N_EXPERTS = 16
CAPACITY_FACTOR = 2
D_EXPERT = D_MODEL
GRID_W = 64
ROPE_THETA = 10000.0
ROPE_AXIS_DIM = HEAD_DIM // 2
Q_BLOCK = 128
EPS = 1e-6

kernel_name = "hybrid_conformer_axialgqa_ec_moe_encoder"


def rms_norm(x, g):
    xf = x.astype(jnp.float32)
    y = xf * lax.rsqrt(jnp.mean(xf * xf, axis=-1, keepdims=True) + EPS)
    return (y * g.astype(jnp.float32)).astype(x.dtype)


def layer_norm(x, g, b):
    xf = x.astype(jnp.float32)
    mu = jnp.mean(xf, axis=-1, keepdims=True)
    xc = xf - mu
    y = xc * lax.rsqrt(jnp.mean(xc * xc, axis=-1, keepdims=True) + EPS)
    return (y * g.astype(jnp.float32) + b.astype(jnp.float32)).astype(x.dtype)


def axial_rope_tables(s):
    n_rows = s // GRID_W
    rr, cc = jnp.meshgrid(jnp.arange(n_rows), jnp.arange(GRID_W), indexing='ij')
    row = rr.reshape(-1).astype(jnp.float32)
    col = cc.reshape(-1).astype(jnp.float32)
    inv = ROPE_THETA ** (-jnp.arange(0, ROPE_AXIS_DIM, 2, dtype=jnp.float32) / ROPE_AXIS_DIM)
    ang = jnp.concatenate([row[:, None] * inv, col[:, None] * inv], axis=-1)
    return jnp.cos(ang), jnp.sin(ang)


def apply_rope(x, cos, sin):
    xf = x.astype(jnp.float32)
    x1, x2 = xf[..., 0::2], xf[..., 1::2]
    c, s_ = cos[None, :, None, :], sin[None, :, None, :]
    out = jnp.stack([x1 * c - x2 * s_, x1 * s_ + x2 * c], axis=-1)
    return out.reshape(x.shape).astype(x.dtype)


def blocked_gqa_attention(q, k, v):
    b, s, _, d = q.shape
    nb = s // Q_BLOCK
    qb = q.reshape(b, nb, Q_BLOCK, N_KV_HEADS, Q_PER_KV, d).transpose(1, 0, 2, 3, 4, 5)
    scale = HEAD_DIM ** -0.5

    def one_block(qblk):
        sc = jnp.einsum('bqkgd,bskd->bkgqs', qblk, k).astype(jnp.float32) * scale
        p = jax.nn.softmax(sc, axis=-1)
        return jnp.einsum('bkgqs,bskd->bqkgd', p.astype(v.dtype), v)

    o = lax.map(one_block, qb)
    return o.transpose(1, 0, 2, 3, 4, 5).reshape(b, s, N_HEADS * d)


def conformer_conv(u, conv_w, conv_b, ln_g, ln_b):
    val, gate = jnp.split(u, 2, axis=-1)
    a = val * jax.nn.sigmoid(gate)
    y = lax.conv_general_dilated(a, conv_w, window_strides=(1,), padding=((CONV_PAD, CONV_PAD),),
                                 dimension_numbers=('NWC', 'WIO', 'NWC'), feature_group_count=D_CONV)
    y = layer_norm(y + conv_b, ln_g, ln_b)
    return jax.nn.silu(y)


def expert_choice_ffn(h, w_router, w_gate, w_up, w_down):
    b, s, d = h.shape
    t = h.reshape(-1, d)
    n = t.shape[0]
    cap = CAPACITY_FACTOR * n // N_EXPERTS
    aff = jax.nn.softmax((t @ w_router).astype(jnp.float32), axis=-1)
    g, idx = lax.top_k(aff.T, cap)
    xe = t[idx]
    hid = jax.nn.silu(jnp.einsum('ecd,edf->ecf', xe, w_gate)) * jnp.einsum('ecd,edf->ecf', xe, w_up)
    ye = jnp.einsum('ecf,efd->ecd', hid, w_down) * g[..., None].astype(t.dtype)
    out = jnp.zeros_like(t).at[idx.reshape(-1)].add(ye.reshape(-1, d))
    return out.reshape(b, s, d)


def encoder_layer(x, c, ada_w, ada_b, norm_mix, w_in, q_norm, k_norm, conv_w, conv_b,
                  conv_ln_g, conv_ln_b, w_out, norm_ffn, w_router, w_gate, w_up, w_down):
    b, s, _ = x.shape
    mod = jax.nn.silu(c) @ ada_w + ada_b
    shift1, scale1, gate1, shift2, scale2, gate2 = [m[:, None, :] for m in jnp.split(mod, 6, axis=-1)]

    h = rms_norm(x, norm_mix) * (1 + scale1) + shift1
    z = h @ w_in
    q, k, v, u = jnp.split(z, [D_ATTN, D_ATTN + D_KV, D_ATTN + 2 * D_KV], axis=-1)
    cos, sin = axial_rope_tables(s)
    q = apply_rope(rms_norm(q.reshape(b, s, N_HEADS, HEAD_DIM), q_norm), cos, sin)
    k = apply_rope(rms_norm(k.reshape(b, s, N_KV_HEADS, HEAD_DIM), k_norm), cos, sin)
    v = v.reshape(b, s, N_KV_HEADS, HEAD_DIM)
    attn_out = blocked_gqa_attention(q, k, v)
    conv_out = conformer_conv(u, conv_w, conv_b, conv_ln_g, conv_ln_b)
    mix = jnp.concatenate([attn_out, conv_out], axis=-1) @ w_out
    x = x + gate1 * mix

    h2 = rms_norm(x, norm_ffn) * (1 + scale2) + shift2
    x = x + gate2 * expert_choice_ffn(h2, w_router, w_gate, w_up, w_down)
    return x


def setup_inputs(seed: int = 0) -> dict:
    key = jax.random.key(seed)
    ks = jax.random.split(key, 24)
    f32 = jnp.float32
    nrm = lambda k, shape, sc: jax.random.normal(k, shape, f32) * sc
    D = D_MODEL
    return {
        "x_prompt": nrm(ks[0], (BATCH, SEQ, D), 1.0),
        "x_sample": nrm(ks[1], (DEC_BATCH, DEC_SEQ, D), 1.0),
        "c_prompt": nrm(ks[2], (BATCH, D), 1.0),
        "c_sample": nrm(ks[3], (DEC_BATCH, D), 1.0),
        "ada_w": nrm(ks[4], (DEPTH, D, 6 * D), 0.5 * D ** -0.5),
        "ada_b": nrm(ks[5], (DEPTH, 6 * D), 0.02),
        "norm_mix": 1.0 + nrm(ks[6], (DEPTH, D), 0.05),
        "w_in": nrm(ks[7], (DEPTH, D, D_IN), D ** -0.5),
        "q_norm": 1.0 + nrm(ks[8], (DEPTH, HEAD_DIM), 0.05),
        "k_norm": 1.0 + nrm(ks[9], (DEPTH, HEAD_DIM), 0.05),
        "conv_w": nrm(ks[10], (DEPTH, CONV_WIDTH, 1, D_CONV), CONV_WIDTH ** -0.5),
        "conv_b": nrm(ks[11], (DEPTH, D_CONV), 0.02),
        "conv_ln_g": 1.0 + nrm(ks[12], (DEPTH, D_CONV), 0.05),
        "conv_ln_b": nrm(ks[13], (DEPTH, D_CONV), 0.02),
        "w_out": nrm(ks[14], (DEPTH, D_MIX, D), D_MIX ** -0.5),
        "norm_ffn": 1.0 + nrm(ks[15], (DEPTH, D), 0.05),
        "w_router": nrm(ks[16], (DEPTH, D, N_EXPERTS), D ** -0.5),
        "w_gate": nrm(ks[17], (DEPTH, N_EXPERTS, D, D_EXPERT), D ** -0.5),
        "w_up": nrm(ks[18], (DEPTH, N_EXPERTS, D, D_EXPERT), D ** -0.5),
        "w_down": nrm(ks[19], (DEPTH, N_EXPERTS, D_EXPERT, D), D_EXPERT ** -0.5),
    }


def reference(x_prompt, x_sample, c_prompt, c_sample, ada_w, ada_b, norm_mix, w_in, q_norm, k_norm,
              conv_w, conv_b, conv_ln_g, conv_ln_b, w_out, norm_ffn, w_router, w_gate, w_up, w_down):
    y_prompt = x_prompt
    y_sample = x_sample
    for l in range(DEPTH):
        p = (ada_w[l], ada_b[l], norm_mix[l], w_in[l], q_norm[l], k_norm[l], conv_w[l], conv_b[l],
             conv_ln_g[l], conv_ln_b[l], w_out[l], norm_ffn[l], w_router[l], w_gate[l], w_up[l], w_down[l])
        y_prompt = encoder_layer(y_prompt, c_prompt, *p)
        y_sample = encoder_layer(y_sample, c_sample, *p)
    return (y_prompt, y_sample)
```

```python
import functools

import jax
import jax.numpy as jnp
from jax import lax
from jax.experimental import pallas as pl
from jax.experimental.pallas import tpu as pltpu

HEAD_DIM = 64
N_KV_HEADS = 2
GRID_W = 64
ROPE_THETA = 10000.0
CAPACITY_FACTOR = 2
EPS = 1e-6

LANES = 128
BF16_SUBLANES = 16
VMEM_LIMIT_BYTES = 48 * 1024 * 1024

TOKEN_TILE = 512
MOE_TILE = 512
SLOT_BLOCK = 128
FFN_TILE = 512
Q_TILE = 128
K_TILE = 512
CONV_TILE = 512
CONV_HALO = 16
CONV_ROWS = 64

f32 = jnp.float32
bf16 = jnp.bfloat16


def _params(*sem):
    return pltpu.CompilerParams(dimension_semantics=sem, vmem_limit_bytes=VMEM_LIMIT_BYTES)


def _silu(x):
    return x * jax.nn.sigmoid(x)


def _ada_kernel(c_ref, w_ref, b_ref, o_ref):
    c = c_ref[...]
    o_ref[...] = jnp.dot(_silu(c), w_ref[...], preferred_element_type=f32,
                         precision=lax.Precision.HIGHEST) + b_ref[...]


def _ada(c, w, b):
    r, d = c.shape
    n = w.shape[1]
    tn = n // 4
    return pl.pallas_call(
        _ada_kernel,
        grid=(n // tn,),
        in_specs=[pl.BlockSpec((r, d), lambda j: (0, 0)),
                  pl.BlockSpec((d, tn), lambda j: (0, j)),
                  pl.BlockSpec((1, tn), lambda j: (0, j))],
        out_specs=pl.BlockSpec((r, tn), lambda j: (0, j)),
        out_shape=jax.ShapeDtypeStruct((r, n), f32),
        compiler_params=_params("arbitrary"),
        name="ada",
    )(c, w, b.reshape(1, n))


def _norm_rope(t, bd_ref, gain_ref, cos, sin, scale):
    ms = jnp.dot((t * t).astype(bf16), bd_ref[...], preferred_element_type=f32)
    tn = t * lax.rsqrt(ms + EPS) * gain_ref[...]
    w = t.shape[1]
    reps = w // LANES
    if reps > 1:
        cos = jnp.concatenate([cos] * reps, axis=1)
        sin = jnp.concatenate([sin] * reps, axis=1)
    lane = lax.broadcasted_iota(jnp.int32, tn.shape, 1)
    first_half = (lane % HEAD_DIM) < (HEAD_DIM // 2)
    partner = jnp.where(first_half, pltpu.roll(tn, w - HEAD_DIM // 2, 1), pltpu.roll(tn, HEAD_DIM // 2, 1))
    return (tn * cos + partner * sin) * scale


def _inproj_kernel(x_ref, sh_ref, sc_ref, g_ref, w_ref, qg_ref, kg_ref, bdq_ref, bdk_ref, cos_ref, sin_ref,
                   q_ref, k_ref, v_ref, u_ref, *, d_attn, d_kv):
    x = x_ref[0]
    ms = jnp.mean(x * x, axis=-1, keepdims=True)
    h = x * lax.rsqrt(ms + EPS) * g_ref[...]
    h = h * (1.0 + sc_ref[0]) + sh_ref[0]
    z = jnp.dot(h.astype(bf16), w_ref[...], preferred_element_type=f32)
    cos = cos_ref[...]
    sin = sin_ref[...]
    q = _norm_rope(z[:, :d_attn], bdq_ref, qg_ref, cos, sin, HEAD_DIM ** -0.5)
    k = _norm_rope(z[:, d_attn:d_attn + d_kv], bdk_ref, kg_ref, cos, sin, 1.0)
    v = z[:, d_attn + d_kv:d_attn + 2 * d_kv]
    q_ref[0] = q.astype(bf16)
    for g in range(d_kv // HEAD_DIM):
        k_ref[0, g] = k[:, g * HEAD_DIM:(g + 1) * HEAD_DIM].astype(bf16)
        v_ref[0, g] = v[:, g * HEAD_DIM:(g + 1) * HEAD_DIM].astype(bf16)
    u_ref[0] = z[:, d_attn + 2 * d_kv:].astype(bf16)


def _inproj(x, shift, scale, gain, w_in, qg, kg, bdq, bdk, cos, sin, d_attn, d_kv):
    b, s, d = x.shape
    d_in = w_in.shape[1]
    d_u = d_in - d_attn - 2 * d_kv
    n_kv = d_kv // HEAD_DIM
    tm = TOKEN_TILE
    const = lambda shape: pl.BlockSpec(shape, lambda bi, i: (0,) * len(shape))
    per_b = pl.BlockSpec((1, 1, d), lambda bi, i: (bi, 0, 0))
    return pl.pallas_call(
        functools.partial(_inproj_kernel, d_attn=d_attn, d_kv=d_kv),
        grid=(b, s // tm),
        in_specs=[pl.BlockSpec((1, tm, d), lambda bi, i: (bi, i, 0)), per_b, per_b,
                  const((1, d)), const((d, d_in)), const((1, d_attn)), const((1, d_kv)),
                  const((d_attn, d_attn)), const((d_kv, d_kv)),
                  pl.BlockSpec((tm, LANES), lambda bi, i: (i, 0)),
                  pl.BlockSpec((tm, LANES), lambda bi, i: (i, 0))],
        out_specs=[pl.BlockSpec((1, tm, d_attn), lambda bi, i: (bi, i, 0)),
                   pl.BlockSpec((1, n_kv, tm, HEAD_DIM), lambda bi, i: (bi, 0, i, 0)),
                   pl.BlockSpec((1, n_kv, tm, HEAD_DIM), lambda bi, i: (bi, 0, i, 0)),
                   pl.BlockSpec((1, tm, d_u), lambda bi, i: (bi, i, 0))],
        out_shape=[jax.ShapeDtypeStruct((b, s, d_attn), bf16),
                   jax.ShapeDtypeStruct((b, n_kv, s, HEAD_DIM), bf16),
                   jax.ShapeDtypeStruct((b, n_kv, s, HEAD_DIM), bf16),
                   jax.ShapeDtypeStruct((b, s, d_u), bf16)],
        compiler_params=_params("arbitrary", "arbitrary"),
        name="inproj",
    )(x, shift, scale, gain, w_in, qg, kg, bdq, bdk, cos, sin)


def _attn_kernel(q_ref, k_ref, v_ref, o_ref, *, tk, n_rep):
    tq = q_ref.shape[1]
    q = q_ref[0]
    qs = jnp.concatenate([q[:, h * HEAD_DIM:(h + 1) * HEAD_DIM] for h in range(n_rep)], axis=0)
    rows = qs.shape[0]
    nk = k_ref.shape[2] // tk

    def body(j, carry):
        m, l, acc = carry
        start = pl.multiple_of(j * tk, tk)
        kt = k_ref[0, 0, pl.ds(start, tk), :]
        vt = v_ref[0, 0, pl.ds(start, tk), :]
        s = lax.dot_general(qs, kt, (((1,), (1,)), ((), ())), preferred_element_type=f32)
        m_new = jnp.maximum(m, jnp.max(s, axis=-1, keepdims=True))
        p = jnp.exp(s - m_new)
        alpha = jnp.exp(m - m_new)
        l = alpha * l + jnp.sum(p, axis=-1, keepdims=True)
        acc = alpha * acc + jnp.dot(p.astype(bf16), vt, preferred_element_type=f32)
        return m_new, l, acc

    m0 = jnp.full((rows, 1), -jnp.inf, f32)
    l0 = jnp.zeros((rows, 1), f32)
    acc0 = jnp.zeros((rows, HEAD_DIM), f32)
    _, l, acc = lax.fori_loop(0, nk, body, (m0, l0, acc0))
    o = acc / l
    o_ref[0] = jnp.concatenate([o[h * tq:(h + 1) * tq] for h in range(n_rep)], axis=1).astype(bf16)


def _attention(q, k, v):
    b, s, d_attn = q.shape
    n_kv = k.shape[1]
    n_rep = d_attn // HEAD_DIM // n_kv
    tq = min(Q_TILE, s)
    tk = min(K_TILE, s)
    gw = n_rep * HEAD_DIM
    return pl.pallas_call(
        functools.partial(_attn_kernel, tk=tk, n_rep=n_rep),
        grid=(b, n_kv, s // tq),
        in_specs=[pl.BlockSpec((1, tq, gw), lambda bi, g, i: (bi, i, g)),
                  pl.BlockSpec((1, 1, s, HEAD_DIM), lambda bi, g, i: (bi, g, 0, 0)),
                  pl.BlockSpec((1, 1, s, HEAD_DIM), lambda bi, g, i: (bi, g, 0, 0))],
        out_specs=pl.BlockSpec((1, tq, gw), lambda bi, g, i: (bi, i, g)),
        out_shape=jax.ShapeDtypeStruct((b, s, d_attn), bf16),
        compiler_params=_params("arbitrary", "arbitrary", "arbitrary"),
        name="attn",
    )(q, k, v)


def _conv_kernel(u_ref, up_ref, un_ref, w_ref, cb_ref, lg_ref, lb_ref, o_ref, a_ref, *, ts, dc, width):
    i = pl.program_id(1)
    last = pl.num_programs(1) - 1
    halo = CONV_HALO

    def glu(u):
        u = u.astype(f32)
        return u[:, :dc] * jax.nn.sigmoid(u[:, dc:])

    a_ref[pl.ds(halo, ts), :] = glu(u_ref[0])
    a_ref[pl.ds(0, halo), :] = jnp.where(i > 0, glu(up_ref[0]), 0.0)
    a_ref[pl.ds(halo + ts, halo), :] = jnp.where(i < last, glu(un_ref[0]), 0.0)

    off = halo - width // 2
    cb = cb_ref[...]
    lg = lg_ref[...]
    lb = lb_ref[...]
    for c in range(ts // CONV_ROWS):
        r0 = c * CONV_ROWS
        acc = jnp.zeros((CONV_ROWS, dc), f32)
        for j in range(width):
            acc = acc + a_ref[pl.ds(r0 + off + j, CONV_ROWS), :] * w_ref[j:j + 1, :]
        y = acc + cb
        mu = jnp.mean(y, axis=-1, keepdims=True)
        yc = y - mu
        var = jnp.mean(yc * yc, axis=-1, keepdims=True)
        yn = yc * lax.rsqrt(var + EPS) * lg + lb
        o_ref[0, pl.ds(r0, CONV_ROWS), :] = _silu(yn).astype(bf16)


def _conv(u, w, cb, lg, lb):
    b, s, du = u.shape
    dc = du // 2
    width = w.shape[0]
    ts = min(CONV_TILE, s)
    hb = ts // CONV_HALO
    n_h = s // CONV_HALO
    const = lambda shape: pl.BlockSpec(shape, lambda bi, i: (0,) * len(shape))
    return pl.pallas_call(
        functools.partial(_conv_kernel, ts=ts, dc=dc, width=width),
        grid=(b, s // ts),
        in_specs=[pl.BlockSpec((1, ts, du), lambda bi, i: (bi, i, 0)),
                  pl.BlockSpec((1, CONV_HALO, du), lambda bi, i: (bi, jnp.maximum(i * hb - 1, 0), 0)),
                  pl.BlockSpec((1, CONV_HALO, du), lambda bi, i: (bi, jnp.minimum((i + 1) * hb, n_h - 1), 0)),
                  const((width, dc)), const((1, dc)), const((1, dc)), const((1, dc))],
        out_specs=pl.BlockSpec((1, ts, dc), lambda bi, i: (bi, i, 0)),
        out_shape=jax.ShapeDtypeStruct((b, s, dc), bf16),
        scratch_shapes=[pltpu.VMEM((ts + 2 * CONV_HALO, dc), f32)],
        compiler_params=_params("arbitrary", "arbitrary"),
        name="conv",
    )(u, u, u, w, cb, lg, lb)


def _outproj_kernel(a_ref, c_ref, x_ref, g1_ref, sc_ref, sh_ref, nf_ref, wa_ref, wc_ref, wr1_ref, wr2_ref,
                    x1_ref, h2_ref, aff_ref, *, ne):
    mix = (jnp.dot(a_ref[0], wa_ref[...], preferred_element_type=f32)
           + jnp.dot(c_ref[0], wc_ref[...], preferred_element_type=f32))
    x1 = x_ref[0] + g1_ref[0] * mix
    x1_ref[0] = x1
    ms = jnp.mean(x1 * x1, axis=-1, keepdims=True)
    h = x1 * lax.rsqrt(ms + EPS) * nf_ref[...]
    h2 = h * (1.0 + sc_ref[0]) + sh_ref[0]
    hi = h2.astype(bf16)
    lo = (h2 - hi.astype(f32)).astype(bf16)
    h2_ref[0] = hi
    r1 = jnp.dot(hi, wr1_ref[...], preferred_element_type=f32)
    r2 = jnp.dot(lo, wr2_ref[...], preferred_element_type=f32)
    logits = r1[:, :ne] + r1[:, ne:] + r2
    m = jnp.max(logits, axis=-1, keepdims=True)
    e = jnp.exp(logits - m)
    aff_ref[0] = e / jnp.sum(e, axis=-1, keepdims=True)


def _outproj(attn, conv, x, gate1, scale2, shift2, norm_ffn, w_a, w_c, wr1, wr2):
    b, s, d = x.shape
    da = attn.shape[2]
    dc = conv.shape[2]
    ne = wr2.shape[1]
    tm = TOKEN_TILE
    const = lambda shape: pl.BlockSpec(shape, lambda bi, i: (0,) * len(shape))
    per_b = pl.BlockSpec((1, 1, d), lambda bi, i: (bi, 0, 0))
    return pl.pallas_call(
        functools.partial(_outproj_kernel, ne=ne),
        grid=(b, s // tm),
        in_specs=[pl.BlockSpec((1, tm, da), lambda bi, i: (bi, i, 0)),
                  pl.BlockSpec((1, tm, dc), lambda bi, i: (bi, i, 0)),
                  pl.BlockSpec((1, tm, d), lambda bi, i: (bi, i, 0)),
                  per_b, per_b, per_b, const((1, d)), const((da, d)), const((dc, d)),
                  const((d, 2 * ne)), const((d, ne))],
        out_specs=[pl.BlockSpec((1, tm, d), lambda bi, i: (bi, i, 0)),
                   pl.BlockSpec((1, tm, d), lambda bi, i: (bi, i, 0)),
                   pl.BlockSpec((1, tm, ne), lambda bi, i: (bi, i, 0))],
        out_shape=[jax.ShapeDtypeStruct((b, s, d), f32),
                   jax.ShapeDtypeStruct((b, s, d), bf16),
                   jax.ShapeDtypeStruct((b, s, ne), f32)],
        compiler_params=_params("arbitrary", "arbitrary"),
        name="outproj",
    )(attn, conv, x, gate1, scale2, shift2, norm_ffn, w_a, w_c, wr1, wr2)


def _thresh_kernel(bits_ref, thr_ref, ntie_ref, *, cap, ne, chunk):
    rows = bits_ref.shape[0]

    def count_ge(cand):
        cand_row = cand[0:1, :]

        def body(i, acc):
            blk = bits_ref[pl.ds(pl.multiple_of(i * chunk, chunk), chunk), :]
            hit = jnp.where(blk >= cand_row, 1, 0).astype(jnp.int32)
            return acc + jnp.sum(hit.reshape(chunk // 8, 8, LANES), axis=0)
        acc = lax.fori_loop(0, rows // chunk, body, jnp.zeros((8, LANES), jnp.int32))
        tot = jnp.sum(acc, axis=0, keepdims=True)
        tot = jnp.broadcast_to(tot, (8, LANES))
        shift = LANES // 2
        while shift >= ne:
            tot = tot + pltpu.roll(tot, shift, 1)
            shift //= 2
        return tot

    def bit_step(i, prefix):
        cand = prefix | jnp.left_shift(jnp.int32(1), 30 - i)
        return jnp.where(count_ge(cand) >= cap, cand, prefix)

    prefix = lax.fori_loop(0, 31, bit_step, jnp.zeros((8, LANES), jnp.int32))
    thr_ref[...] = prefix
    ntie_ref[...] = cap - count_ge(prefix + 1)


def _thresh(aff_bits, cap, ne):
    rows = aff_bits.shape[0]
    chunk = min(512, rows)
    return pl.pallas_call(
        functools.partial(_thresh_kernel, cap=cap, ne=ne, chunk=chunk),
        out_shape=[jax.ShapeDtypeStruct((8, LANES), jnp.int32)] * 2,
        compiler_params=pltpu.CompilerParams(vmem_limit_bytes=VMEM_LIMIT_BYTES),
        name="thresh",
    )(aff_bits)


def _plan_kernel(aff_ref, thr_ref, ntie_ref, ut_ref, slot_ref, cnt_ref, base_ref, used_ref, base_s, tie_s):
    @pl.when(pl.program_id(0) == 0)
    def _():
        base_s[...] = jnp.zeros_like(base_s)
        tie_s[...] = jnp.zeros_like(tie_s)

    a = aff_ref[...]
    thr = thr_ref[...]
    eq = a == thr
    eqf = jnp.where(eq, 1.0, 0.0)
    ut = ut_ref[...]
    tie_rank = tie_s[...] + jnp.dot(eqf.astype(bf16), ut, preferred_element_type=f32)
    sel = (a > thr) | (eq & (tie_rank < ntie_ref[...]))
    self_ = jnp.where(sel, 1.0, 0.0)
    rank = jnp.dot(self_.astype(bf16), ut, preferred_element_type=f32)
    cnt = jnp.sum(self_, axis=1, keepdims=True)
    base = base_s[...]
    slot_ref[...] = jnp.where(sel, base + rank, -1.0)
    cnt_ref[0] = cnt.astype(jnp.int32)
    base_ref[0] = base.astype(jnp.int32)
    new_base = base + jnp.ceil(cnt * (1.0 / BF16_SUBLANES)) * BF16_SUBLANES
    base_s[...] = new_base
    used_ref[...] = new_base.astype(jnp.int32)
    tie_s[...] = tie_s[...] + jnp.sum(eqf, axis=1, keepdims=True)


def _plan(aff_t, thr, ntie, ut):
    ne, n = aff_t.shape
    t = MOE_TILE
    tiles = n // t
    return pl.pallas_call(
        _plan_kernel,
        grid=(tiles,),
        in_specs=[pl.BlockSpec((ne, t), lambda i: (0, i)),
                  pl.BlockSpec((ne, 1), lambda i: (0, 0)),
                  pl.BlockSpec((ne, 1), lambda i: (0, 0)),
                  pl.BlockSpec((t, t), lambda i: (0, 0))],
        out_specs=[pl.BlockSpec((ne, t), lambda i: (0, i)),
                   pl.BlockSpec((1, ne, 1), lambda i: (i, 0, 0)),
                   pl.BlockSpec((1, ne, 1), lambda i: (i, 0, 0)),
                   pl.BlockSpec((ne, 1), lambda i: (0, 0))],
        out_shape=[jax.ShapeDtypeStruct((ne, n), f32),
                   jax.ShapeDtypeStruct((tiles, ne, 1), jnp.int32),
                   jax.ShapeDtypeStruct((tiles, ne, 1), jnp.int32),
                   jax.ShapeDtypeStruct((ne, 1), jnp.int32)],
        scratch_shapes=[pltpu.VMEM((ne, 1), f32), pltpu.VMEM((ne, 1), f32)],
        compiler_params=_params("arbitrary"),
        name="plan",
    )(aff_t, thr, ntie, ut)


def _blocks_needed(cnt_sm, tile, ne):
    mx = cnt_sm[tile * ne]
    for e in range(1, ne):
        mx = jnp.maximum(mx, cnt_sm[tile * ne + e])
    return (mx + SLOT_BLOCK - 1) // SLOT_BLOCK


def _gather_copy(stage, xe_hbm, sem, buf, e, start):
    r = SLOT_BLOCK
    return pltpu.make_async_copy(stage.at[buf, pl.ds(e * r, r), :], xe_hbm.at[e, pl.ds(start, r), :], sem.at[buf])


def _gather_kernel(base_sm, cnt_sm, used_sm, slot_ref, h_ref, xe_hbm, sel_s, stage, zeros_s, sem, state,
                   *, ne, group):
    i = pl.program_id(0)
    r = SLOT_BLOCK
    t = slot_ref.shape[1]
    rows_alloc = xe_hbm.shape[1]

    @pl.when(i == 0)
    def _():
        state[0] = 0
        state[1] = 0

    def wait_buf(buf):
        for e in range(ne):
            _gather_copy(stage, xe_hbm, sem, buf, e, 0).wait()

    def block(b, _):
        buf = state[0]
        row_iota = lax.broadcasted_iota(jnp.int32, (r, t), 0).astype(f32)
        for e in range(ne):
            start = (base_sm[i * ne + e] + b * r).astype(f32)
            sel_s[pl.ds(e * r, r), :] = jnp.where(slot_ref[e:e + 1, :] == row_iota + start, 1.0, 0.0).astype(bf16)
        h = h_ref[...]
        for c in range(ne // group):
            rows = pl.ds(c * group * r, group * r)
            stage[buf, rows, :] = jnp.dot(sel_s[rows, :], h, preferred_element_type=f32).astype(bf16)

        @pl.when(state[1] == 1)
        def _():
            wait_buf(1 - buf)

        for e in range(ne):
            start = pl.multiple_of(base_sm[i * ne + e] + b * r, BF16_SUBLANES)
            _gather_copy(stage, xe_hbm, sem, buf, e, start).start()
        state[0] = 1 - buf
        state[1] = 1
        return 0

    lax.fori_loop(0, _blocks_needed(cnt_sm, i, ne), block, 0)

    @pl.when((i == pl.num_programs(0) - 1) & (state[1] == 1))
    def _():
        wait_buf(1 - state[0])

    @pl.when(i == pl.num_programs(0) - 1)
    def _():
        zeros_s[...] = jnp.zeros_like(zeros_s)

        def fill_copy(e, start, rows):
            return pltpu.make_async_copy(zeros_s.at[pl.ds(0, rows), :], xe_hbm.at[e, pl.ds(start, rows), :],
                                         sem.at[2])

        small = BF16_SUBLANES
        for e in range(ne):
            used = used_sm[e]
            n_big = (rows_alloc - used) // r
            n_small = (rows_alloc - used - n_big * r) // small

            def fill_big(k, _, e=e, used=used):
                fill_copy(e, pl.multiple_of(used + k * r, small), r).start()
                return 0

            def fill_small(k, _, e=e, used=used, n_big=n_big):
                fill_copy(e, pl.multiple_of(used + n_big * r + k * small, small), small).start()
                return 0

            def drain_big(k, _, e=e):
                fill_copy(e, 0, r).wait()
                return 0

            def drain_small(k, _, e=e):
                fill_copy(e, 0, small).wait()
                return 0

            lax.fori_loop(0, n_big, fill_big, 0)
            lax.fori_loop(0, n_small, fill_small, 0)
            lax.fori_loop(0, n_big, drain_big, 0)
            lax.fori_loop(0, n_small, drain_small, 0)


def _gather(base, cnt, used, slot_t, h2, rows_alloc):
    ne, n = slot_t.shape
    d = h2.shape[1]
    t = MOE_TILE
    group = 4
    return pl.pallas_call(
        functools.partial(_gather_kernel, ne=ne, group=group),
        grid_spec=pltpu.PrefetchScalarGridSpec(
            num_scalar_prefetch=3,
            grid=(n // t,),
            in_specs=[pl.BlockSpec((ne, t), lambda i, *_: (0, i)),
                      pl.BlockSpec((t, d), lambda i, *_: (i, 0))],
            out_specs=pl.BlockSpec(memory_space=pl.ANY),
            scratch_shapes=[pltpu.VMEM((ne * SLOT_BLOCK, t), bf16),
                            pltpu.VMEM((2, ne * SLOT_BLOCK, d), bf16),
                            pltpu.VMEM((SLOT_BLOCK, d), bf16),
                            pltpu.SemaphoreType.DMA((3,)),
                            pltpu.SMEM((2,), jnp.int32)]),
        out_shape=jax.ShapeDtypeStruct((ne, rows_alloc, d), bf16),
        compiler_params=_params("arbitrary"),
        name="gather",
    )(base, cnt, used, slot_t, h2)


def _ffn_kernel(used_sm, x_ref, wg_ref, wu_ref, wd_ref, o_ref):
    e = pl.program_id(0)
    tm = x_ref.shape[1]
    r0 = pl.program_id(1) * tm
    used = used_sm[e]

    @pl.when(r0 < used)
    def _():
        x = x_ref[0]
        a = jnp.dot(x, wg_ref[0], preferred_element_type=f32)
        b = jnp.dot(x, wu_ref[0], preferred_element_type=f32)
        hid = (_silu(a) * b).astype(bf16)
        o_ref[0] = jnp.dot(hid, wd_ref[0], preferred_element_type=f32).astype(bf16)

    @pl.when(r0 >= used)
    def _():
        o_ref[...] = jnp.zeros_like(o_ref)


def _ffn(used, xe, wg, wu, wd):
    ne, rows, d = xe.shape
    f = wg.shape[2]
    tm = FFN_TILE
    return pl.pallas_call(
        _ffn_kernel,
        grid_spec=pltpu.PrefetchScalarGridSpec(
            num_scalar_prefetch=1,
            grid=(ne, rows // tm),
            in_specs=[pl.BlockSpec((1, tm, d), lambda e, i, *_: (e, i, 0)),
                      pl.BlockSpec((1, d, f), lambda e, i, *_: (e, 0, 0)),
                      pl.BlockSpec((1, d, f), lambda e, i, *_: (e, 0, 0)),
                      pl.BlockSpec((1, f, d), lambda e, i, *_: (e, 0, 0))],
            out_specs=pl.BlockSpec((1, tm, d), lambda e, i, *_: (e, i, 0))),
        out_shape=jax.ShapeDtypeStruct((ne, rows, d), bf16),
        compiler_params=_params("arbitrary", "arbitrary"),
        name="ffn",
    )(used, xe, wg, wu, wd)


def _combine_copy(ye_hbm, ybuf, sem, buf, e, start):
    r = SLOT_BLOCK
    return pltpu.make_async_copy(ye_hbm.at[e, pl.ds(start, r), :], ybuf.at[buf, pl.ds(e * r, r), :], sem.at[buf])


def _combine_kernel(base_sm, cnt_sm, slot_ref, aff_ref, x1_ref, g2_ref, ye_hbm, o_ref, sel_s, ybuf, sem, *, ne):
    i = pl.program_id(0)
    last = pl.num_programs(0) - 1
    r = SLOT_BLOCK
    t = slot_ref.shape[0]

    def start_block(tile, b, buf):
        for e in range(ne):
            start = pl.multiple_of(base_sm[tile * ne + e] + b * r, BF16_SUBLANES)
            _combine_copy(ye_hbm, ybuf, sem, buf, e, start).start()

    def wait_block(buf):
        for e in range(ne):
            _combine_copy(ye_hbm, ybuf, sem, buf, e, 0).wait()

    def one_hot(tile, b):
        lane_iota = lax.broadcasted_iota(jnp.int32, (t, r), 1).astype(f32)
        for e in range(ne):
            start = (base_sm[tile * ne + e] + b * r).astype(f32)
            hit = slot_ref[:, e:e + 1] == lane_iota + start
            sel_s[:, pl.ds(e * r, r)] = jnp.where(hit, aff_ref[:, e:e + 1], 0.0).astype(bf16)

    @pl.when(i == 0)
    def _():
        start_block(0, 0, 0)

    @pl.when(i < last)
    def _():
        start_block(i + 1, 0, (i + 1) % 2)

    buf = i % 2
    one_hot(i, 0)
    wait_block(buf)
    moe = jnp.dot(sel_s[...], ybuf[buf], preferred_element_type=f32)

    def extra(b, acc):
        start_block(i, b, 2)
        one_hot(i, b)
        wait_block(2)
        return acc + jnp.dot(sel_s[...], ybuf[2], preferred_element_type=f32)

    moe = lax.fori_loop(1, jnp.maximum(_blocks_needed(cnt_sm, i, ne), 1), extra, moe)
    o_ref[...] = x1_ref[...] + g2_ref[0] * moe


def _combine(base, cnt, slot, aff, x1, gate2, ye, seq):
    n, ne = slot.shape
    d = x1.shape[1]
    t = MOE_TILE
    per_seq = seq // t
    return pl.pallas_call(
        functools.partial(_combine_kernel, ne=ne),
        grid_spec=pltpu.PrefetchScalarGridSpec(
            num_scalar_prefetch=2,
            grid=(n // t,),
            in_specs=[pl.BlockSpec((t, ne), lambda i, *_: (i, 0)),
                      pl.BlockSpec((t, ne), lambda i, *_: (i, 0)),
                      pl.BlockSpec((t, d), lambda i, *_: (i, 0)),
                      pl.BlockSpec((1, 1, d), lambda i, *_: (i // per_seq, 0, 0)),
                      pl.BlockSpec(memory_space=pl.ANY)],
            out_specs=pl.BlockSpec((t, d), lambda i, *_: (i, 0)),
            scratch_shapes=[pltpu.VMEM((t, ne * SLOT_BLOCK), bf16),
                            pltpu.VMEM((3, ne * SLOT_BLOCK, d), bf16),
                            pltpu.SemaphoreType.DMA((3,))]),
        out_shape=jax.ShapeDtypeStruct((n, d), f32),
        compiler_params=_params("arbitrary"),
        name="combine",
    )(base, cnt, slot, aff, x1, gate2, ye)


def _rope_tables(s):
    t = jnp.arange(s)
    row = (t // GRID_W).astype(f32)
    col = (t % GRID_W).astype(f32)
    axis_dim = HEAD_DIM // 2
    inv = ROPE_THETA ** (-jnp.arange(0, axis_dim, 2, dtype=f32) / axis_dim)
    ang = jnp.concatenate([row[:, None] * inv, col[:, None] * inv], axis=-1)
    cos, sin = jnp.cos(ang), jnp.sin(ang)
    reps = LANES // HEAD_DIM
    cos_t = jnp.tile(jnp.concatenate([cos, cos], axis=-1), (1, reps))
    sin_t = jnp.tile(jnp.concatenate([-sin, sin], axis=-1), (1, reps))
    return cos_t, sin_t


def _head_perm(n_heads):
    half = jnp.concatenate([jnp.arange(0, HEAD_DIM, 2), jnp.arange(1, HEAD_DIM, 2)])
    return (jnp.arange(n_heads)[:, None] * HEAD_DIM + half[None, :]).reshape(-1)


def _block_mean(width):
    idx = jnp.arange(width) // HEAD_DIM
    return jnp.where(idx[:, None] == idx[None, :], 1.0 / HEAD_DIM, 0.0).astype(bf16)


def _prepare(norm_mix, w_in, q_norm, k_norm, conv_w, conv_b, conv_ln_g, conv_ln_b, w_out, norm_ffn,
             w_router, w_gate, w_up, w_down):
    d = w_in.shape[0]
    d_conv = conv_w.shape[-1]
    d_attn = w_out.shape[0] - d_conv
    d_kv = (w_in.shape[1] - d_attn - 2 * d_conv) // 2
    n_heads = d_attn // HEAD_DIM
    n_kv = d_kv // HEAD_DIM
    qp = _head_perm(n_heads)
    kp = _head_perm(n_kv)
    cols = jnp.concatenate([qp, d_attn + kp, jnp.arange(d_attn + d_kv, w_in.shape[1])])
    wr_hi = w_router.astype(bf16)
    wr_lo = (w_router - wr_hi.astype(f32)).astype(bf16)
    return dict(
        d_attn=d_attn, d_kv=d_kv,
        norm_mix=norm_mix.reshape(1, d),
        w_in=w_in[:, cols].astype(bf16),
        qg=jnp.tile(q_norm[_head_perm(1)], n_heads).reshape(1, d_attn),
        kg=jnp.tile(k_norm[_head_perm(1)], n_kv).reshape(1, d_kv),
        bdq=_block_mean(d_attn), bdk=_block_mean(d_kv),
        conv_w=conv_w.reshape(conv_w.shape[0], d_conv), conv_b=conv_b.reshape(1, d_conv),
        conv_ln_g=conv_ln_g.reshape(1, d_conv), conv_ln_b=conv_ln_b.reshape(1, d_conv),
        w_a=w_out[:d_attn].astype(bf16), w_c=w_out[d_attn:].astype(bf16),
        norm_ffn=norm_ffn.reshape(1, d),
        wr1=jnp.concatenate([wr_hi, wr_lo], axis=1), wr2=wr_hi,
        w_gate=w_gate.astype(bf16), w_up=w_up.astype(bf16), w_down=w_down.astype(bf16),
    )


def _encoder_layer(x, mod, p):
    b, s, d = x.shape
    shift1, scale1, gate1, shift2, scale2, gate2 = [m.reshape(b, 1, d) for m in jnp.split(mod, 6, axis=-1)]
    cos, sin = _rope_tables(s)
    q, k, v, u = _inproj(x, shift1, scale1, p["norm_mix"], p["w_in"], p["qg"], p["kg"], p["bdq"], p["bdk"],
                         cos, sin, p["d_attn"], p["d_kv"])
    attn = _attention(q, k, v)
    conv = _conv(u, p["conv_w"], p["conv_b"], p["conv_ln_g"], p["conv_ln_b"])
    x1, h2, aff = _outproj(attn, conv, x, gate1, scale2, shift2, p["norm_ffn"], p["w_a"], p["w_c"],
                           p["wr1"], p["wr2"])

    n = b * s
    ne = aff.shape[-1]
    cap = CAPACITY_FACTOR * n // ne
    aff = aff.reshape(n, ne)
    bits = lax.bitcast_convert_type(aff, jnp.int32).reshape(n * ne // LANES, LANES)
    thr_bits, ntie = _thresh(bits, cap, ne)
    thr = lax.bitcast_convert_type(thr_bits[0, :ne], f32).reshape(ne, 1)
    ntie = ntie[0, :ne].astype(f32).reshape(ne, 1)
    t = MOE_TILE
    tok = jnp.arange(t)
    ut = (tok[:, None] < tok[None, :]).astype(bf16)
    slot_t, cnt, base, used = _plan(aff.T, thr, ntie, ut)
    cnt = cnt.reshape(-1)
    base = base.reshape(-1)
    tiles = n // t
    rows_alloc = -(-(cap + BF16_SUBLANES * tiles + SLOT_BLOCK) // FFN_TILE) * FFN_TILE
    used = used.reshape(-1)
    xe = _gather(base, cnt, used, slot_t, h2.reshape(n, d), rows_alloc)
    ye = _ffn(used, xe, p["w_gate"], p["w_up"], p["w_down"])
    out = _combine(base, cnt, slot_t.T, aff, x1.reshape(n, d), gate2, ye, s)
    return out.reshape(b, s, d)


def kernel(x_prompt, x_sample, c_prompt, c_sample, ada_w, ada_b, norm_mix, w_in, q_norm, k_norm, conv_w, conv_b,
           conv_ln_g, conv_ln_b, w_out, norm_ffn, w_router, w_gate, w_up, w_down):
    y_prompt, y_sample = x_prompt, x_sample
    nb = x_prompt.shape[0]
    for l in range(ada_w.shape[0]):
        p = _prepare(norm_mix[l], w_in[l], q_norm[l], k_norm[l], conv_w[l], conv_b[l], conv_ln_g[l],
                     conv_ln_b[l], w_out[l], norm_ffn[l], w_router[l], w_gate[l], w_up[l], w_down[l])
        mod = _ada(jnp.concatenate([c_prompt, c_sample], axis=0), ada_w[l], ada_b[l])
        y_prompt = _encoder_layer(y_prompt, mod[:nb], p)
        y_sample = _encoder_layer(y_sample, mod[nb:], p)
    return (y_prompt, y_sample)
```

```python
import functools

import jax
import jax.numpy as jnp
from jax import lax
from jax.experimental import pallas as pl
from jax.experimental.pallas import tpu as pltpu

HEAD_DIM = 64
N_KV_HEADS = 2
GRID_W = 64
ROPE_THETA = 10000.0
CAPACITY_FACTOR = 2
EPS = 1e-6
LOG2E = 1.4426950408889634

LANES = 128
BF16_SUBLANES = 16
VMEM_LIMIT_BYTES = 48 * 1024 * 1024

TOKEN_TILE = 512
MOE_TILE = 512
SLOT_BLOCK = 128
FFN_TILE = 512
Q_TILE = 256
K_TILE = 256
CONV_TILE = 512
CONV_HALO = 16
CONV_ROWS = 64

f32 = jnp.float32
bf16 = jnp.bfloat16


def _params(*sem):
    return pltpu.CompilerParams(dimension_semantics=sem, vmem_limit_bytes=VMEM_LIMIT_BYTES)


def _silu(x):
    return x * jax.nn.sigmoid(x)


def _ada_kernel(c_ref, w_ref, b_ref, o_ref):
    c = c_ref[...]
    o_ref[...] = jnp.dot(_silu(c), w_ref[...], preferred_element_type=f32,
                         precision=lax.Precision.HIGHEST) + b_ref[...]


def _ada(c, w, b):
    r, d = c.shape
    n = w.shape[1]
    tn = n // 4
    return pl.pallas_call(
        _ada_kernel,
        grid=(n // tn,),
        in_specs=[pl.BlockSpec((r, d), lambda j: (0, 0)),
                  pl.BlockSpec((d, tn), lambda j: (0, j)),
                  pl.BlockSpec((1, tn), lambda j: (0, j))],
        out_specs=pl.BlockSpec((r, tn), lambda j: (0, j)),
        out_shape=jax.ShapeDtypeStruct((r, n), f32),
        compiler_params=_params("arbitrary"),
        name="ada",
    )(c, w, b.reshape(1, n))


def _norm_rope(t, bd_ref, gain_ref, cos, sin, scale):
    ms = jnp.dot((t * t).astype(bf16), bd_ref[...], preferred_element_type=f32)
    tn = t * lax.rsqrt(ms + EPS) * gain_ref[...]
    w = t.shape[1]
    reps = w // LANES
    if reps > 1:
        cos = jnp.concatenate([cos] * reps, axis=1)
        sin = jnp.concatenate([sin] * reps, axis=1)
    lane = lax.broadcasted_iota(jnp.int32, tn.shape, 1)
    first_half = (lane % HEAD_DIM) < (HEAD_DIM // 2)
    partner = jnp.where(first_half, pltpu.roll(tn, w - HEAD_DIM // 2, 1), pltpu.roll(tn, HEAD_DIM // 2, 1))
    return (tn * cos + partner * sin) * scale


def _inproj_kernel(x_ref, sh_ref, sc_ref, g_ref, w_ref, qg_ref, kg_ref, bdq_ref, bdk_ref, cos_ref, sin_ref,
                   q_ref, k_ref, v_ref, u_ref, *, d_attn, d_kv):
    x = x_ref[0]
    ms = jnp.mean(x * x, axis=-1, keepdims=True)
    h = x * lax.rsqrt(ms + EPS) * g_ref[...]
    h = h * (1.0 + sc_ref[0]) + sh_ref[0]
    z = jnp.dot(h.astype(bf16), w_ref[...], preferred_element_type=f32)
    cos = cos_ref[...]
    sin = sin_ref[...]
    q = _norm_rope(z[:, :d_attn], bdq_ref, qg_ref, cos, sin, HEAD_DIM ** -0.5 * LOG2E)
    k = _norm_rope(z[:, d_attn:d_attn + d_kv], bdk_ref, kg_ref, cos, sin, 1.0)
    v = z[:, d_attn + d_kv:d_attn + 2 * d_kv]
    q_ref[0] = q.T.astype(bf16)
    vt = v.T.astype(bf16)
    for g in range(d_kv // HEAD_DIM):
        k_ref[0, g] = k[:, g * HEAD_DIM:(g + 1) * HEAD_DIM].astype(bf16)
        for j in range(v_ref.shape[2]):
            v_ref[0, g, j] = vt[g * HEAD_DIM:(g + 1) * HEAD_DIM, j * K_TILE:(j + 1) * K_TILE]
    u_ref[0] = z[:, d_attn + 2 * d_kv:].astype(bf16)


def _inproj(x, shift, scale, gain, w_in, qg, kg, bdq, bdk, cos, sin, d_attn, d_kv):
    b, s, d = x.shape
    d_in = w_in.shape[1]
    d_u = d_in - d_attn - 2 * d_kv
    n_kv = d_kv // HEAD_DIM
    tm = TOKEN_TILE
    const = lambda shape: pl.BlockSpec(shape, lambda bi, i: (0,) * len(shape))
    per_b = pl.BlockSpec((1, 1, d), lambda bi, i: (bi, 0, 0))
    return pl.pallas_call(
        functools.partial(_inproj_kernel, d_attn=d_attn, d_kv=d_kv),
        grid=(b, s // tm),
        in_specs=[pl.BlockSpec((1, tm, d), lambda bi, i: (bi, i, 0)), per_b, per_b,
                  const((1, d)), const((d, d_in)), const((1, d_attn)), const((1, d_kv)),
                  const((d_attn, d_attn)), const((d_kv, d_kv)),
                  pl.BlockSpec((tm, LANES), lambda bi, i: (i, 0)),
                  pl.BlockSpec((tm, LANES), lambda bi, i: (i, 0))],
        out_specs=[pl.BlockSpec((1, d_attn, tm), lambda bi, i: (bi, 0, i)),
                   pl.BlockSpec((1, n_kv, tm, HEAD_DIM), lambda bi, i: (bi, 0, i, 0)),
                   pl.BlockSpec((1, n_kv, tm // K_TILE, HEAD_DIM, K_TILE), lambda bi, i: (bi, 0, i, 0, 0)),
                   pl.BlockSpec((1, tm, d_u), lambda bi, i: (bi, i, 0))],
        out_shape=[jax.ShapeDtypeStruct((b, d_attn, s), bf16),
                   jax.ShapeDtypeStruct((b, n_kv, s, HEAD_DIM), bf16),
                   jax.ShapeDtypeStruct((b, n_kv, s // K_TILE, HEAD_DIM, K_TILE), bf16),
                   jax.ShapeDtypeStruct((b, s, d_u), bf16)],
        compiler_params=_params("arbitrary", "arbitrary"),
        name="inproj",
    )(x, shift, scale, gain, w_in, qg, kg, bdq, bdk, cos, sin)


def _attn_kernel(q_ref, k_ref, v_ref, o_ref, s_scr, p_scr, *, n_rep):
    tq = q_ref.shape[2]
    n_blk, _, tk = v_ref.shape[2:]
    qt = q_ref[0]
    qs = jnp.concatenate([qt[h * HEAD_DIM:(h + 1) * HEAD_DIM, :] for h in range(n_rep)], axis=1)
    cols = qs.shape[1]
    ones = jnp.ones((BF16_SUBLANES, tk), bf16)

    def scores(i, slot):
        kt = k_ref[0, 0, pl.ds(pl.multiple_of(jnp.minimum(i, n_blk - 1) * tk, tk), tk), :]
        s_scr[slot] = jnp.dot(kt, qs, preferred_element_type=f32)

    def softmax(slot, m):
        m_new = jnp.maximum(m, jnp.max(s_scr[slot], axis=0, keepdims=True))
        p_scr[slot] = jnp.exp2(s_scr[slot] - m_new).astype(bf16)
        return m_new, jnp.exp2(m - m_new)

    def pv(i, slot):
        return jnp.dot(jnp.concatenate([v_ref[0, 0, i], ones], axis=0), p_scr[slot], preferred_element_type=f32)

    def body(j, carry):
        m, alpha, acc = carry
        i = 2 * j
        for slot in range(2):
            out = pv(i + slot, slot)
            scores(i + slot + 2, slot)
            m, alpha_next = softmax(1 - slot, m)
            acc = alpha * acc + out
            alpha = alpha_next
        return m, alpha, acc

    scores(0, 0)
    m, alpha = softmax(0, jnp.full((1, cols), -jnp.inf, f32))
    scores(1, 1)
    acc = jnp.zeros((HEAD_DIM + BF16_SUBLANES, cols), f32)
    _, _, acc = lax.fori_loop(0, n_blk // 2, body, (m, alpha, acc))
    o = acc[:HEAD_DIM] / acc[HEAD_DIM:HEAD_DIM + 1]
    o_ref[0] = jnp.concatenate([o[:, h * tq:(h + 1) * tq] for h in range(n_rep)], axis=0).astype(bf16)


def _attention(qt, k, vt):
    b, d_attn, s = qt.shape
    n_kv, n_blk, _, tk = vt.shape[1:]
    n_rep = d_attn // HEAD_DIM // n_kv
    tq = min(Q_TILE, s)
    gw = n_rep * HEAD_DIM
    return pl.pallas_call(
        functools.partial(_attn_kernel, n_rep=n_rep),
        grid=(b, n_kv, s // tq),
        in_specs=[pl.BlockSpec((1, gw, tq), lambda bi, g, i: (bi, g, i)),
                  pl.BlockSpec((1, 1, s, HEAD_DIM), lambda bi, g, i: (bi, g, 0, 0)),
                  pl.BlockSpec((1, 1, n_blk, HEAD_DIM, tk), lambda bi, g, i: (bi, g, 0, 0, 0))],
        out_specs=pl.BlockSpec((1, gw, tq), lambda bi, g, i: (bi, g, i)),
        out_shape=jax.ShapeDtypeStruct((b, d_attn, s), bf16),
        scratch_shapes=[pltpu.VMEM((2, tk, n_rep * tq), f32), pltpu.VMEM((2, tk, n_rep * tq), bf16)],
        compiler_params=_params("arbitrary", "arbitrary", "arbitrary"),
        name="attn",
    )(qt, k, vt)


def _conv_kernel(u_ref, up_ref, un_ref, w_ref, cb_ref, lg_ref, lb_ref, o_ref, a_ref, y_ref, *, ts, dc, width):
    i = pl.program_id(1)
    last = pl.num_programs(1) - 1
    halo = CONV_HALO

    def glu(u):
        u = u.astype(f32)
        return u[:, :dc] * jax.nn.sigmoid(u[:, dc:])

    a_ref[pl.ds(halo, ts), :] = glu(u_ref[0])
    a_ref[pl.ds(0, halo), :] = jnp.where(i > 0, glu(up_ref[0]), 0.0)
    a_ref[pl.ds(halo + ts, halo), :] = jnp.where(i < last, glu(un_ref[0]), 0.0)

    off = halo - width // 2
    sub = 8
    rows = CONV_ROWS
    reach = -(-(off + width) // sub) * sub
    cb = cb_ref[...]
    lg = lg_ref[...]
    lb = lb_ref[...]
    for c in range(ts // rows):
        r0 = c * rows
        for g in range(dc // LANES):
            lanes = pl.ds(g * LANES, LANES)
            blk = a_ref[pl.ds(r0, rows + reach), lanes]
            acc = None
            for r in range(sub):
                part = None
                for u in range(r, off + width, sub):
                    if u < off:
                        continue
                    term = blk[u - r:u - r + rows + sub, :] * w_ref[pl.ds(u - off, 1), lanes]
                    part = term if part is None else part + term
                shifted = part[r:r + rows, :]
                acc = shifted if acc is None else acc + shifted
            y_ref[pl.ds(r0, rows), lanes] = acc
        y = y_ref[pl.ds(r0, rows), :] + cb
        mu = jnp.mean(y, axis=-1, keepdims=True)
        yc = y - mu
        var = jnp.mean(yc * yc, axis=-1, keepdims=True)
        yn = yc * lax.rsqrt(var + EPS) * lg + lb
        o_ref[0, pl.ds(r0, CONV_ROWS), :] = _silu(yn).astype(bf16)


def _conv(u, w, cb, lg, lb):
    b, s, du = u.shape
    dc = du // 2
    width = w.shape[0]
    ts = min(CONV_TILE, s)
    hb = ts // CONV_HALO
    n_h = s // CONV_HALO
    const = lambda shape: pl.BlockSpec(shape, lambda bi, i: (0,) * len(shape))
    return pl.pallas_call(
        functools.partial(_conv_kernel, ts=ts, dc=dc, width=width),
        grid=(b, s // ts),
        in_specs=[pl.BlockSpec((1, ts, du), lambda bi, i: (bi, i, 0)),
                  pl.BlockSpec((1, CONV_HALO, du), lambda bi, i: (bi, jnp.maximum(i * hb - 1, 0), 0)),
                  pl.BlockSpec((1, CONV_HALO, du), lambda bi, i: (bi, jnp.minimum((i + 1) * hb, n_h - 1), 0)),
                  const((width, dc)), const((1, dc)), const((1, dc)), const((1, dc))],
        out_specs=pl.BlockSpec((1, ts, dc), lambda bi, i: (bi, i, 0)),
        out_shape=jax.ShapeDtypeStruct((b, s, dc), bf16),
        scratch_shapes=[pltpu.VMEM((ts + 2 * CONV_HALO, dc), f32), pltpu.VMEM((ts, dc), f32)],
        compiler_params=_params("arbitrary", "arbitrary"),
        name="conv",
    )(u, u, u, w, cb, lg, lb)


def _outproj_kernel(a_ref, c_ref, x_ref, g1_ref, sc_ref, sh_ref, nf_ref, wa_ref, wc_ref, wr1_ref, wr2_ref,
                    x1_ref, h2_ref, aff_ref, *, ne):
    attn = a_ref[0].astype(f32).T.astype(bf16)
    mix = (jnp.dot(attn, wa_ref[...], preferred_element_type=f32)
           + jnp.dot(c_ref[0], wc_ref[...], preferred_element_type=f32))
    x1 = x_ref[0] + g1_ref[0] * mix
    x1_ref[0] = x1
    ms = jnp.mean(x1 * x1, axis=-1, keepdims=True)
    h = x1 * lax.rsqrt(ms + EPS) * nf_ref[...]
    h2 = h * (1.0 + sc_ref[0]) + sh_ref[0]
    hi = h2.astype(bf16)
    lo = (h2 - hi.astype(f32)).astype(bf16)
    h2_ref[0] = hi
    r1 = jnp.dot(hi, wr1_ref[...], preferred_element_type=f32)
    r2 = jnp.dot(lo, wr2_ref[...], preferred_element_type=f32)
    logits = r1[:, :ne] + r1[:, ne:] + r2
    m = jnp.max(logits, axis=-1, keepdims=True)
    e = jnp.exp(logits - m)
    aff_ref[0] = e / jnp.sum(e, axis=-1, keepdims=True)


def _outproj(attn, conv, x, gate1, scale2, shift2, norm_ffn, w_a, w_c, wr1, wr2):
    b, s, d = x.shape
    da = attn.shape[1]
    dc = conv.shape[2]
    ne = wr2.shape[1]
    tm = TOKEN_TILE
    const = lambda shape: pl.BlockSpec(shape, lambda bi, i: (0,) * len(shape))
    per_b = pl.BlockSpec((1, 1, d), lambda bi, i: (bi, 0, 0))
    return pl.pallas_call(
        functools.partial(_outproj_kernel, ne=ne),
        grid=(b, s // tm),
        in_specs=[pl.BlockSpec((1, da, tm), lambda bi, i: (bi, 0, i)),
                  pl.BlockSpec((1, tm, dc), lambda bi, i: (bi, i, 0)),
                  pl.BlockSpec((1, tm, d), lambda bi, i: (bi, i, 0)),
                  per_b, per_b, per_b, const((1, d)), const((da, d)), const((dc, d)),
                  const((d, 2 * ne)), const((d, ne))],
        out_specs=[pl.BlockSpec((1, tm, d), lambda bi, i: (bi, i, 0)),
                   pl.BlockSpec((1, tm, d), lambda bi, i: (bi, i, 0)),
                   pl.BlockSpec((1, tm, ne), lambda bi, i: (bi, i, 0))],
        out_shape=[jax.ShapeDtypeStruct((b, s, d), f32),
                   jax.ShapeDtypeStruct((b, s, d), bf16),
                   jax.ShapeDtypeStruct((b, s, ne), f32)],
        compiler_params=_params("arbitrary", "arbitrary"),
        name="outproj",
    )(attn, conv, x, gate1, scale2, shift2, norm_ffn, w_a, w_c, wr1, wr2)


def _thresh_kernel(bits_ref, thr_ref, ntie_ref, *, cap, ne, chunk):
    rows = bits_ref.shape[0]

    def count_ge(cand):
        cand_row = cand[0:1, :]

        def body(i, acc):
            blk = bits_ref[pl.ds(pl.multiple_of(i * chunk, chunk), chunk), :]
            hit = jnp.where(blk >= cand_row, 1, 0).astype(jnp.int32)
            return acc + jnp.sum(hit.reshape(chunk // 8, 8, LANES), axis=0)
        acc = lax.fori_loop(0, rows // chunk, body, jnp.zeros((8, LANES), jnp.int32))
        tot = jnp.sum(acc, axis=0, keepdims=True)
        tot = jnp.broadcast_to(tot, (8, LANES))
        shift = LANES // 2
        while shift >= ne:
            tot = tot + pltpu.roll(tot, shift, 1)
            shift //= 2
        return tot

    def bit_step(i, prefix):
        cand = prefix | jnp.left_shift(jnp.int32(1), 30 - i)
        return jnp.where(count_ge(cand) >= cap, cand, prefix)

    prefix = lax.fori_loop(0, 31, bit_step, jnp.zeros((8, LANES), jnp.int32))
    thr_ref[...] = prefix
    ntie_ref[...] = cap - count_ge(prefix + 1)


def _thresh(aff_bits, cap, ne):
    rows = aff_bits.shape[0]
    chunk = min(512, rows)
    return pl.pallas_call(
        functools.partial(_thresh_kernel, cap=cap, ne=ne, chunk=chunk),
        out_shape=[jax.ShapeDtypeStruct((8, LANES), jnp.int32)] * 2,
        compiler_params=pltpu.CompilerParams(vmem_limit_bytes=VMEM_LIMIT_BYTES),
        name="thresh",
    )(aff_bits)


def _plan_kernel(aff_ref, thr_ref, ntie_ref, ut_ref, slot_ref, cnt_ref, base_ref, used_ref, base_s, tie_s):
    @pl.when(pl.program_id(0) == 0)
    def _():
        base_s[...] = jnp.zeros_like(base_s)
        tie_s[...] = jnp.zeros_like(tie_s)

    a = aff_ref[...]
    thr = thr_ref[...]
    eq = a == thr
    eqf = jnp.where(eq, 1.0, 0.0)
    ut = ut_ref[...]
    tie_rank = tie_s[...] + jnp.dot(eqf.astype(bf16), ut, preferred_element_type=f32)
    sel = (a > thr) | (eq & (tie_rank < ntie_ref[...]))
    self_ = jnp.where(sel, 1.0, 0.0)
    rank = jnp.dot(self_.astype(bf16), ut, preferred_element_type=f32)
    cnt = jnp.sum(self_, axis=1, keepdims=True)
    base = base_s[...]
    slot_ref[...] = jnp.where(sel, base + rank, -1.0)
    cnt_ref[0] = cnt.astype(jnp.int32)
    base_ref[0] = base.astype(jnp.int32)
    new_base = base + jnp.ceil(cnt * (1.0 / BF16_SUBLANES)) * BF16_SUBLANES
    base_s[...] = new_base
    used_ref[...] = new_base.astype(jnp.int32)
    tie_s[...] = tie_s[...] + jnp.sum(eqf, axis=1, keepdims=True)


def _plan(aff_t, thr, ntie, ut):
    ne, n = aff_t.shape
    t = MOE_TILE
    tiles = n // t
    return pl.pallas_call(
        _plan_kernel,
        grid=(tiles,),
        in_specs=[pl.BlockSpec((ne, t), lambda i: (0, i)),
                  pl.BlockSpec((ne, 1), lambda i: (0, 0)),
                  pl.BlockSpec((ne, 1), lambda i: (0, 0)),
                  pl.BlockSpec((t, t), lambda i: (0, 0))],
        out_specs=[pl.BlockSpec((ne, t), lambda i: (0, i)),
                   pl.BlockSpec((1, ne, 1), lambda i: (i, 0, 0)),
                   pl.BlockSpec((1, ne, 1), lambda i: (i, 0, 0)),
                   pl.BlockSpec((ne, 1), lambda i: (0, 0))],
        out_shape=[jax.ShapeDtypeStruct((ne, n), f32),
                   jax.ShapeDtypeStruct((tiles, ne, 1), jnp.int32),
                   jax.ShapeDtypeStruct((tiles, ne, 1), jnp.int32),
                   jax.ShapeDtypeStruct((ne, 1), jnp.int32)],
        scratch_shapes=[pltpu.VMEM((ne, 1), f32), pltpu.VMEM((ne, 1), f32)],
        compiler_params=_params("arbitrary"),
        name="plan",
    )(aff_t, thr, ntie, ut)


def _blocks_needed(cnt_sm, tile, ne):
    mx = cnt_sm[tile * ne]
    for e in range(1, ne):
        mx = jnp.maximum(mx, cnt_sm[tile * ne + e])
    return (mx + SLOT_BLOCK - 1) // SLOT_BLOCK


def _gather_copy(stage, xe_hbm, sem, buf, e, start):
    r = SLOT_BLOCK
    return pltpu.make_async_copy(stage.at[buf, pl.ds(e * r, r), :], xe_hbm.at[e, pl.ds(start, r), :], sem.at[buf])


def _gather_kernel(base_sm, cnt_sm, used_sm, slot_ref, h_ref, xe_hbm, sel_s, stage, zeros_s, sem, state,
                   *, ne, group):
    i = pl.program_id(0)
    r = SLOT_BLOCK
    t = slot_ref.shape[1]
    rows_alloc = xe_hbm.shape[1]

    @pl.when(i == 0)
    def _():
        state[0] = 0
        state[1] = 0

    def wait_buf(buf):
        for e in range(ne):
            _gather_copy(stage, xe_hbm, sem, buf, e, 0).wait()

    def block(b, _):
        buf = state[0]
        row_iota = lax.broadcasted_iota(jnp.int32, (r, t), 0).astype(f32)
        for e in range(ne):
            start = (base_sm[i * ne + e] + b * r).astype(f32)
            sel_s[pl.ds(e * r, r), :] = jnp.where(slot_ref[e:e + 1, :] == row_iota + start, 1.0, 0.0).astype(bf16)
        h = h_ref[...]
        for c in range(ne // group):
            rows = pl.ds(c * group * r, group * r)
            stage[buf, rows, :] = jnp.dot(sel_s[rows, :], h, preferred_element_type=f32).astype(bf16)

        @pl.when(state[1] == 1)
        def _():
            wait_buf(1 - buf)

        for e in range(ne):
            start = pl.multiple_of(base_sm[i * ne + e] + b * r, BF16_SUBLANES)
            _gather_copy(stage, xe_hbm, sem, buf, e, start).start()
        state[0] = 1 - buf
        state[1] = 1
        return 0

    lax.fori_loop(0, _blocks_needed(cnt_sm, i, ne), block, 0)

    @pl.when((i == pl.num_programs(0) - 1) & (state[1] == 1))
    def _():
        wait_buf(1 - state[0])

    @pl.when(i == pl.num_programs(0) - 1)
    def _():
        zeros_s[...] = jnp.zeros_like(zeros_s)

        def fill_copy(e, start, rows):
            return pltpu.make_async_copy(zeros_s.at[pl.ds(0, rows), :], xe_hbm.at[e, pl.ds(start, rows), :],
                                         sem.at[2])

        small = BF16_SUBLANES
        for e in range(ne):
            used = used_sm[e]
            n_big = (rows_alloc - used) // r
            n_small = (rows_alloc - used - n_big * r) // small

            def fill_big(k, _, e=e, used=used):
                fill_copy(e, pl.multiple_of(used + k * r, small), r).start()
                return 0

            def fill_small(k, _, e=e, used=used, n_big=n_big):
                fill_copy(e, pl.multiple_of(used + n_big * r + k * small, small), small).start()
                return 0

            def drain_big(k, _, e=e):
                fill_copy(e, 0, r).wait()
                return 0

            def drain_small(k, _, e=e):
                fill_copy(e, 0, small).wait()
                return 0

            lax.fori_loop(0, n_big, fill_big, 0)
            lax.fori_loop(0, n_small, fill_small, 0)
            lax.fori_loop(0, n_big, drain_big, 0)
            lax.fori_loop(0, n_small, drain_small, 0)


def _gather(base, cnt, used, slot_t, h2, rows_alloc):
    ne, n = slot_t.shape
    d = h2.shape[1]
    t = MOE_TILE
    group = 4
    return pl.pallas_call(
        functools.partial(_gather_kernel, ne=ne, group=group),
        grid_spec=pltpu.PrefetchScalarGridSpec(
            num_scalar_prefetch=3,
            grid=(n // t,),
            in_specs=[pl.BlockSpec((ne, t), lambda i, *_: (0, i)),
                      pl.BlockSpec((t, d), lambda i, *_: (i, 0))],
            out_specs=pl.BlockSpec(memory_space=pl.ANY),
            scratch_shapes=[pltpu.VMEM((ne * SLOT_BLOCK, t), bf16),
                            pltpu.VMEM((2, ne * SLOT_BLOCK, d), bf16),
                            pltpu.VMEM((SLOT_BLOCK, d), bf16),
                            pltpu.SemaphoreType.DMA((3,)),
                            pltpu.SMEM((2,), jnp.int32)]),
        out_shape=jax.ShapeDtypeStruct((ne, rows_alloc, d), bf16),
        compiler_params=_params("arbitrary"),
        name="gather",
    )(base, cnt, used, slot_t, h2)


def _ffn_kernel(used_sm, x_ref, wg_ref, wu_ref, wd_ref, o_ref):
    e = pl.program_id(0)
    tm = x_ref.shape[1]
    r0 = pl.program_id(1) * tm
    used = used_sm[e]

    @pl.when(r0 < used)
    def _():
        x = x_ref[0]
        a = jnp.dot(x, wg_ref[0], preferred_element_type=f32)
        b = jnp.dot(x, wu_ref[0], preferred_element_type=f32)
        hid = (_silu(a) * b).astype(bf16)
        o_ref[0] = jnp.dot(hid, wd_ref[0], preferred_element_type=f32).astype(bf16)

    @pl.when(r0 >= used)
    def _():
        o_ref[...] = jnp.zeros_like(o_ref)


def _ffn(used, xe, wg, wu, wd):
    ne, rows, d = xe.shape
    f = wg.shape[2]
    tm = FFN_TILE
    return pl.pallas_call(
        _ffn_kernel,
        grid_spec=pltpu.PrefetchScalarGridSpec(
            num_scalar_prefetch=1,
            grid=(ne, rows // tm),
            in_specs=[pl.BlockSpec((1, tm, d), lambda e, i, *_: (e, i, 0)),
                      pl.BlockSpec((1, d, f), lambda e, i, *_: (e, 0, 0)),
                      pl.BlockSpec((1, d, f), lambda e, i, *_: (e, 0, 0)),
                      pl.BlockSpec((1, f, d), lambda e, i, *_: (e, 0, 0))],
            out_specs=pl.BlockSpec((1, tm, d), lambda e, i, *_: (e, i, 0))),
        out_shape=jax.ShapeDtypeStruct((ne, rows, d), bf16),
        compiler_params=_params("arbitrary", "arbitrary"),
        name="ffn",
    )(used, xe, wg, wu, wd)


def _combine_copy(ye_hbm, ybuf, sem, buf, e, start):
    r = SLOT_BLOCK
    return pltpu.make_async_copy(ye_hbm.at[e, pl.ds(start, r), :], ybuf.at[buf, pl.ds(e * r, r), :], sem.at[buf])


def _combine_kernel(base_sm, cnt_sm, slot_ref, aff_ref, x1_ref, g2_ref, ye_hbm, o_ref, sel_s, ybuf, sem, *, ne):
    i = pl.program_id(0)
    last = pl.num_programs(0) - 1
    r = SLOT_BLOCK
    t = slot_ref.shape[0]

    def start_block(tile, b, buf):
        for e in range(ne):
            start = pl.multiple_of(base_sm[tile * ne + e] + b * r, BF16_SUBLANES)
            _combine_copy(ye_hbm, ybuf, sem, buf, e, start).start()

    def wait_block(buf):
        for e in range(ne):
            _combine_copy(ye_hbm, ybuf, sem, buf, e, 0).wait()

    def one_hot(tile, b):
        lane_iota = lax.broadcasted_iota(jnp.int32, (t, r), 1).astype(f32)
        for e in range(ne):
            start = (base_sm[tile * ne + e] + b * r).astype(f32)
            hit = slot_ref[:, e:e + 1] == lane_iota + start
            sel_s[:, pl.ds(e * r, r)] = jnp.where(hit, aff_ref[:, e:e + 1], 0.0).astype(bf16)

    @pl.when(i == 0)
    def _():
        start_block(0, 0, 0)

    @pl.when(i < last)
    def _():
        start_block(i + 1, 0, (i + 1) % 2)

    buf = i % 2
    one_hot(i, 0)
    wait_block(buf)
    moe = jnp.dot(sel_s[...], ybuf[buf], preferred_element_type=f32)

    def extra(b, acc):
        start_block(i, b, 2)
        one_hot(i, b)
        wait_block(2)
        return acc + jnp.dot(sel_s[...], ybuf[2], preferred_element_type=f32)

    moe = lax.fori_loop(1, jnp.maximum(_blocks_needed(cnt_sm, i, ne), 1), extra, moe)
    o_ref[...] = x1_ref[...] + g2_ref[0] * moe


def _combine(base, cnt, slot, aff, x1, gate2, ye, seq):
    n, ne = slot.shape
    d = x1.shape[1]
    t = MOE_TILE
    per_seq = seq // t
    return pl.pallas_call(
        functools.partial(_combine_kernel, ne=ne),
        grid_spec=pltpu.PrefetchScalarGridSpec(
            num_scalar_prefetch=2,
            grid=(n // t,),
            in_specs=[pl.BlockSpec((t, ne), lambda i, *_: (i, 0)),
                      pl.BlockSpec((t, ne), lambda i, *_: (i, 0)),
                      pl.BlockSpec((t, d), lambda i, *_: (i, 0)),
                      pl.BlockSpec((1, 1, d), lambda i, *_: (i // per_seq, 0, 0)),
                      pl.BlockSpec(memory_space=pl.ANY)],
            out_specs=pl.BlockSpec((t, d), lambda i, *_: (i, 0)),
            scratch_shapes=[pltpu.VMEM((t, ne * SLOT_BLOCK), bf16),
                            pltpu.VMEM((3, ne * SLOT_BLOCK, d), bf16),
                            pltpu.SemaphoreType.DMA((3,))]),
        out_shape=jax.ShapeDtypeStruct((n, d), f32),
        compiler_params=_params("arbitrary"),
        name="combine",
    )(base, cnt, slot, aff, x1, gate2, ye)


def _rope_tables(s):
    t = jnp.arange(s)
    row = (t // GRID_W).astype(f32)
    col = (t % GRID_W).astype(f32)
    axis_dim = HEAD_DIM // 2
    inv = ROPE_THETA ** (-jnp.arange(0, axis_dim, 2, dtype=f32) / axis_dim)
    ang = jnp.concatenate([row[:, None] * inv, col[:, None] * inv], axis=-1)
    cos, sin = jnp.cos(ang), jnp.sin(ang)
    reps = LANES // HEAD_DIM
    cos_t = jnp.tile(jnp.concatenate([cos, cos], axis=-1), (1, reps))
    sin_t = jnp.tile(jnp.concatenate([-sin, sin], axis=-1), (1, reps))
    return cos_t, sin_t


def _head_perm(n_heads):
    half = jnp.concatenate([jnp.arange(0, HEAD_DIM, 2), jnp.arange(1, HEAD_DIM, 2)])
    return (jnp.arange(n_heads)[:, None] * HEAD_DIM + half[None, :]).reshape(-1)


def _block_mean(width):
    idx = jnp.arange(width) // HEAD_DIM
    return jnp.where(idx[:, None] == idx[None, :], 1.0 / HEAD_DIM, 0.0).astype(bf16)


def _prepare(norm_mix, w_in, q_norm, k_norm, conv_w, conv_b, conv_ln_g, conv_ln_b, w_out, norm_ffn,
             w_router, w_gate, w_up, w_down):
    d = w_in.shape[0]
    d_conv = conv_w.shape[-1]
    d_attn = w_out.shape[0] - d_conv
    d_kv = (w_in.shape[1] - d_attn - 2 * d_conv) // 2
    n_heads = d_attn // HEAD_DIM
    n_kv = d_kv // HEAD_DIM
    qp = _head_perm(n_heads)
    kp = _head_perm(n_kv)
    cols = jnp.concatenate([qp, d_attn + kp, jnp.arange(d_attn + d_kv, w_in.shape[1])])
    wr_hi = w_router.astype(bf16)
    wr_lo = (w_router - wr_hi.astype(f32)).astype(bf16)
    return dict(
        d_attn=d_attn, d_kv=d_kv,
        norm_mix=norm_mix.reshape(1, d),
        w_in=w_in[:, cols].astype(bf16),
        qg=jnp.tile(q_norm[_head_perm(1)], n_heads).reshape(1, d_attn),
        kg=jnp.tile(k_norm[_head_perm(1)], n_kv).reshape(1, d_kv),
        bdq=_block_mean(d_attn), bdk=_block_mean(d_kv),
        conv_w=conv_w.reshape(conv_w.shape[0], d_conv), conv_b=conv_b.reshape(1, d_conv),
        conv_ln_g=conv_ln_g.reshape(1, d_conv), conv_ln_b=conv_ln_b.reshape(1, d_conv),
        w_a=w_out[:d_attn].astype(bf16), w_c=w_out[d_attn:].astype(bf16),
        norm_ffn=norm_ffn.reshape(1, d),
        wr1=jnp.concatenate([wr_hi, wr_lo], axis=1), wr2=wr_hi,
        w_gate=w_gate.astype(bf16), w_up=w_up.astype(bf16), w_down=w_down.astype(bf16),
    )


def _encoder_layer(x, mod, p):
    b, s, d = x.shape
    shift1, scale1, gate1, shift2, scale2, gate2 = [m.reshape(b, 1, d) for m in jnp.split(mod, 6, axis=-1)]
    cos, sin = _rope_tables(s)
    q, k, v, u = _inproj(x, shift1, scale1, p["norm_mix"], p["w_in"], p["qg"], p["kg"], p["bdq"], p["bdk"],
                         cos, sin, p["d_attn"], p["d_kv"])
    attn = _attention(q, k, v)
    conv = _conv(u, p["conv_w"], p["conv_b"], p["conv_ln_g"], p["conv_ln_b"])
    x1, h2, aff = _outproj(attn, conv, x, gate1, scale2, shift2, p["norm_ffn"], p["w_a"], p["w_c"],
                           p["wr1"], p["wr2"])

    n = b * s
    ne = aff.shape[-1]
    cap = CAPACITY_FACTOR * n // ne
    aff = aff.reshape(n, ne)
    bits = lax.bitcast_convert_type(aff, jnp.int32).reshape(n * ne // LANES, LANES)
    thr_bits, ntie = _thresh(bits, cap, ne)
    thr = lax.bitcast_convert_type(thr_bits[0, :ne], f32).reshape(ne, 1)
    ntie = ntie[0, :ne].astype(f32).reshape(ne, 1)
    t = MOE_TILE
    tok = jnp.arange(t)
    ut = (tok[:, None] < tok[None, :]).astype(bf16)
    slot_t, cnt, base, used = _plan(aff.T, thr, ntie, ut)
    cnt = cnt.reshape(-1)
    base = base.reshape(-1)
    tiles = n // t
    rows_alloc = -(-(cap + BF16_SUBLANES * tiles + SLOT_BLOCK) // FFN_TILE) * FFN_TILE
    used = used.reshape(-1)
    xe = _gather(base, cnt, used, slot_t, h2.reshape(n, d), rows_alloc)
    ye = _ffn(used, xe, p["w_gate"], p["w_up"], p["w_down"])
    out = _combine(base, cnt, slot_t.T, aff, x1.reshape(n, d), gate2, ye, s)
    return out.reshape(b, s, d)


def kernel(x_prompt, x_sample, c_prompt, c_sample, ada_w, ada_b, norm_mix, w_in, q_norm, k_norm, conv_w, conv_b,
           conv_ln_g, conv_ln_b, w_out, norm_ffn, w_router, w_gate, w_up, w_down):
    y_prompt, y_sample = x_prompt, x_sample
    nb = x_prompt.shape[0]
    for l in range(ada_w.shape[0]):
        p = _prepare(norm_mix[l], w_in[l], q_norm[l], k_norm[l], conv_w[l], conv_b[l], conv_ln_g[l],
                     conv_ln_b[l], w_out[l], norm_ffn[l], w_router[l], w_gate[l], w_up[l], w_down[l])
        mod = _ada(jnp.concatenate([c_prompt, c_sample], axis=0), ada_w[l], ada_b[l])
        y_prompt = _encoder_layer(y_prompt, mod[:nb], p)
        y_sample = _encoder_layer(y_sample, mod[nb:], p)
    return (y_prompt, y_sample)
```

```python
import functools

import jax
import jax.numpy as jnp
from jax import lax
from jax.experimental import pallas as pl
from jax.experimental.pallas import tpu as pltpu

HEAD_DIM = 64
N_KV_HEADS = 2
GRID_W = 64
ROPE_THETA = 10000.0
CAPACITY_FACTOR = 2
EPS = 1e-6
LOG2E = 1.4426950408889634

LANES = 128
BF16_SUBLANES = 16
VMEM_LIMIT_BYTES = 48 * 1024 * 1024

TOKEN_TILE = 512
MOE_TILE = 512
SLOT_BLOCK = 128
FFN_TILE = 512
Q_TILE = 256
K_TILE = 256
ATTN_BLOCKS_PER_TRIP = 8
CONV_TILE = 512
CONV_HALO = 16
CONV_ROWS = 64

f32 = jnp.float32
bf16 = jnp.bfloat16


def _params(*sem):
    return pltpu.CompilerParams(dimension_semantics=sem, vmem_limit_bytes=VMEM_LIMIT_BYTES)


def _silu(x):
    return x * jax.nn.sigmoid(x)


def _ada_kernel(c_ref, w_ref, b_ref, o_ref):
    c = c_ref[...]
    o_ref[...] = jnp.dot(_silu(c), w_ref[...], preferred_element_type=f32,
                         precision=lax.Precision.HIGHEST) + b_ref[...]


def _ada(c, w, b):
    r, d = c.shape
    n = w.shape[1]
    tn = n // 4
    return pl.pallas_call(
        _ada_kernel,
        grid=(n // tn,),
        in_specs=[pl.BlockSpec((r, d), lambda j: (0, 0)),
                  pl.BlockSpec((d, tn), lambda j: (0, j)),
                  pl.BlockSpec((1, tn), lambda j: (0, j))],
        out_specs=pl.BlockSpec((r, tn), lambda j: (0, j)),
        out_shape=jax.ShapeDtypeStruct((r, n), f32),
        compiler_params=_params("arbitrary"),
        name="ada",
    )(c, w, b.reshape(1, n))


def _norm_rope(t, bd_ref, gain_ref, cos, sin, scale):
    ms = jnp.dot((t * t).astype(bf16), bd_ref[...], preferred_element_type=f32)
    tn = t * lax.rsqrt(ms + EPS) * gain_ref[...]
    w = t.shape[1]
    reps = w // LANES
    if reps > 1:
        cos = jnp.concatenate([cos] * reps, axis=1)
        sin = jnp.concatenate([sin] * reps, axis=1)
    lane = lax.broadcasted_iota(jnp.int32, tn.shape, 1)
    first_half = (lane % HEAD_DIM) < (HEAD_DIM // 2)
    partner = jnp.where(first_half, pltpu.roll(tn, w - HEAD_DIM // 2, 1), pltpu.roll(tn, HEAD_DIM // 2, 1))
    return (tn * cos + partner * sin) * scale


def _inproj_kernel(x_ref, sh_ref, sc_ref, g_ref, w_ref, qg_ref, kg_ref, bdq_ref, bdk_ref, cos_ref, sin_ref,
                   q_ref, k_ref, v_ref, u_ref, *, d_attn, d_kv):
    x = x_ref[0]
    ms = jnp.mean(x * x, axis=-1, keepdims=True)
    h = x * lax.rsqrt(ms + EPS) * g_ref[...]
    h = h * (1.0 + sc_ref[0]) + sh_ref[0]
    z = jnp.dot(h.astype(bf16), w_ref[...], preferred_element_type=f32)
    cos = cos_ref[...]
    sin = sin_ref[...]
    q = _norm_rope(z[:, :d_attn], bdq_ref, qg_ref, cos, sin, HEAD_DIM ** -0.5 * LOG2E)
    k = _norm_rope(z[:, d_attn:d_attn + d_kv], bdk_ref, kg_ref, cos, sin, 1.0)
    v = z[:, d_attn + d_kv:d_attn + 2 * d_kv]
    q_ref[0] = q.T.astype(bf16)
    vt = v.T.astype(bf16)
    for g in range(d_kv // HEAD_DIM):
        k_ref[0, g] = k[:, g * HEAD_DIM:(g + 1) * HEAD_DIM].astype(bf16)
        for j in range(v_ref.shape[2]):
            v_ref[0, g, j] = vt[g * HEAD_DIM:(g + 1) * HEAD_DIM, j * K_TILE:(j + 1) * K_TILE]
    u_ref[0] = z[:, d_attn + 2 * d_kv:].astype(bf16)


def _inproj(x, shift, scale, gain, w_in, qg, kg, bdq, bdk, cos, sin, d_attn, d_kv):
    b, s, d = x.shape
    d_in = w_in.shape[1]
    d_u = d_in - d_attn - 2 * d_kv
    n_kv = d_kv // HEAD_DIM
    tm = TOKEN_TILE
    const = lambda shape: pl.BlockSpec(shape, lambda bi, i: (0,) * len(shape))
    per_b = pl.BlockSpec((1, 1, d), lambda bi, i: (bi, 0, 0))
    return pl.pallas_call(
        functools.partial(_inproj_kernel, d_attn=d_attn, d_kv=d_kv),
        grid=(b, s // tm),
        in_specs=[pl.BlockSpec((1, tm, d), lambda bi, i: (bi, i, 0)), per_b, per_b,
                  const((1, d)), const((d, d_in)), const((1, d_attn)), const((1, d_kv)),
                  const((d_attn, d_attn)), const((d_kv, d_kv)),
                  pl.BlockSpec((tm, LANES), lambda bi, i: (i, 0)),
                  pl.BlockSpec((tm, LANES), lambda bi, i: (i, 0))],
        out_specs=[pl.BlockSpec((1, d_attn, tm), lambda bi, i: (bi, 0, i)),
                   pl.BlockSpec((1, n_kv, tm, HEAD_DIM), lambda bi, i: (bi, 0, i, 0)),
                   pl.BlockSpec((1, n_kv, tm // K_TILE, HEAD_DIM, K_TILE), lambda bi, i: (bi, 0, i, 0, 0)),
                   pl.BlockSpec((1, tm, d_u), lambda bi, i: (bi, i, 0))],
        out_shape=[jax.ShapeDtypeStruct((b, d_attn, s), bf16),
                   jax.ShapeDtypeStruct((b, n_kv, s, HEAD_DIM), bf16),
                   jax.ShapeDtypeStruct((b, n_kv, s // K_TILE, HEAD_DIM, K_TILE), bf16),
                   jax.ShapeDtypeStruct((b, s, d_u), bf16)],
        compiler_params=_params("arbitrary", "arbitrary"),
        name="inproj",
    )(x, shift, scale, gain, w_in, qg, kg, bdq, bdk, cos, sin)


def _attn_kernel(q_ref, k_ref, v_ref, o_ref, s_scr, p_scr, *, n_rep):
    tq = q_ref.shape[2]
    n_blk, _, tk = v_ref.shape[2:]
    qt = q_ref[0]
    qs = jnp.concatenate([qt[h * HEAD_DIM:(h + 1) * HEAD_DIM, :] for h in range(n_rep)], axis=1)
    cols = qs.shape[1]
    ones = jnp.ones((BF16_SUBLANES, tk), bf16)

    def scores(i, slot):
        kt = k_ref[0, 0, pl.ds(pl.multiple_of(jnp.minimum(i, n_blk - 1) * tk, tk), tk), :]
        s_scr[slot] = jnp.dot(kt, qs, preferred_element_type=f32)

    def softmax(slot, m):
        m_new = jnp.maximum(m, jnp.max(s_scr[slot], axis=0, keepdims=True))
        p_scr[slot] = jnp.exp2(s_scr[slot] - m_new).astype(bf16)
        return m_new, jnp.exp2(m - m_new)

    def pv(i, slot):
        return jnp.dot(jnp.concatenate([v_ref[0, 0, i], ones], axis=0), p_scr[slot], preferred_element_type=f32)

    per_trip = min(ATTN_BLOCKS_PER_TRIP, n_blk)

    def body(j, carry):
        m, alpha, acc = carry
        for step in range(per_trip):
            i = j * per_trip + step
            slot = step % 2
            out = pv(i, slot)
            scores(i + 2, slot)
            m, alpha_next = softmax(1 - slot, m)
            acc = alpha * acc + out
            alpha = alpha_next
        return m, alpha, acc

    scores(0, 0)
    m, alpha = softmax(0, jnp.full((1, cols), -jnp.inf, f32))
    scores(1, 1)
    acc = jnp.zeros((HEAD_DIM + BF16_SUBLANES, cols), f32)
    _, _, acc = lax.fori_loop(0, n_blk // per_trip, body, (m, alpha, acc))
    o = acc[:HEAD_DIM] / acc[HEAD_DIM:HEAD_DIM + 1]
    o_ref[0] = jnp.concatenate([o[:, h * tq:(h + 1) * tq] for h in range(n_rep)], axis=0).astype(bf16)


def _attention(qt, k, vt):
    b, d_attn, s = qt.shape
    n_kv, n_blk, _, tk = vt.shape[1:]
    n_rep = d_attn // HEAD_DIM // n_kv
    tq = min(Q_TILE, s)
    gw = n_rep * HEAD_DIM
    return pl.pallas_call(
        functools.partial(_attn_kernel, n_rep=n_rep),
        grid=(b, n_kv, s // tq),
        in_specs=[pl.BlockSpec((1, gw, tq), lambda bi, g, i: (bi, g, i)),
                  pl.BlockSpec((1, 1, s, HEAD_DIM), lambda bi, g, i: (bi, g, 0, 0)),
                  pl.BlockSpec((1, 1, n_blk, HEAD_DIM, tk), lambda bi, g, i: (bi, g, 0, 0, 0))],
        out_specs=pl.BlockSpec((1, gw, tq), lambda bi, g, i: (bi, g, i)),
        out_shape=jax.ShapeDtypeStruct((b, d_attn, s), bf16),
        scratch_shapes=[pltpu.VMEM((2, tk, n_rep * tq), f32), pltpu.VMEM((2, tk, n_rep * tq), bf16)],
        compiler_params=_params("arbitrary", "arbitrary", "arbitrary"),
        name="attn",
    )(qt, k, vt)


def _conv_kernel(u_ref, up_ref, un_ref, w_ref, cb_ref, lg_ref, lb_ref, o_ref, a_ref, y_ref, *, ts, dc, width):
    i = pl.program_id(1)
    last = pl.num_programs(1) - 1
    halo = CONV_HALO

    def glu(u):
        u = u.astype(f32)
        return u[:, :dc] * jax.nn.sigmoid(u[:, dc:])

    a_ref[pl.ds(halo, ts), :] = glu(u_ref[0])
    a_ref[pl.ds(0, halo), :] = jnp.where(i > 0, glu(up_ref[0]), 0.0)
    a_ref[pl.ds(halo + ts, halo), :] = jnp.where(i < last, glu(un_ref[0]), 0.0)

    off = halo - width // 2
    sub = 8
    rows = CONV_ROWS
    reach = -(-(off + width) // sub) * sub
    cb = cb_ref[...]
    lg = lg_ref[...]
    lb = lb_ref[...]
    for c in range(ts // rows):
        r0 = c * rows
        for g in range(dc // LANES):
            lanes = pl.ds(g * LANES, LANES)
            blk = a_ref[pl.ds(r0, rows + reach), lanes]
            acc = None
            for r in range(sub):
                part = None
                for u in range(r, off + width, sub):
                    if u < off:
                        continue
                    term = blk[u - r:u - r + rows + sub, :] * w_ref[pl.ds(u - off, 1), lanes]
                    part = term if part is None else part + term
                shifted = part[r:r + rows, :]
                acc = shifted if acc is None else acc + shifted
            y_ref[pl.ds(r0, rows), lanes] = acc
        y = y_ref[pl.ds(r0, rows), :] + cb
        mu = jnp.mean(y, axis=-1, keepdims=True)
        yc = y - mu
        var = jnp.mean(yc * yc, axis=-1, keepdims=True)
        yn = yc * lax.rsqrt(var + EPS) * lg + lb
        o_ref[0, pl.ds(r0, CONV_ROWS), :] = _silu(yn).astype(bf16)


def _conv(u, w, cb, lg, lb):
    b, s, du = u.shape
    dc = du // 2
    width = w.shape[0]
    ts = min(CONV_TILE, s)
    hb = ts // CONV_HALO
    n_h = s // CONV_HALO
    const = lambda shape: pl.BlockSpec(shape, lambda bi, i: (0,) * len(shape))
    return pl.pallas_call(
        functools.partial(_conv_kernel, ts=ts, dc=dc, width=width),
        grid=(b, s // ts),
        in_specs=[pl.BlockSpec((1, ts, du), lambda bi, i: (bi, i, 0)),
                  pl.BlockSpec((1, CONV_HALO, du), lambda bi, i: (bi, jnp.maximum(i * hb - 1, 0), 0)),
                  pl.BlockSpec((1, CONV_HALO, du), lambda bi, i: (bi, jnp.minimum((i + 1) * hb, n_h - 1), 0)),
                  const((width, dc)), const((1, dc)), const((1, dc)), const((1, dc))],
        out_specs=pl.BlockSpec((1, ts, dc), lambda bi, i: (bi, i, 0)),
        out_shape=jax.ShapeDtypeStruct((b, s, dc), bf16),
        scratch_shapes=[pltpu.VMEM((ts + 2 * CONV_HALO, dc), f32), pltpu.VMEM((ts, dc), f32)],
        compiler_params=_params("arbitrary", "arbitrary"),
        name="conv",
    )(u, u, u, w, cb, lg, lb)


def _outproj_kernel(a_ref, c_ref, x_ref, g1_ref, sc_ref, sh_ref, nf_ref, wa_ref, wc_ref, wr1_ref, wr2_ref,
                    x1_ref, h2_ref, aff_ref, *, ne):
    attn = a_ref[0].astype(f32).T.astype(bf16)
    mix = (jnp.dot(attn, wa_ref[...], preferred_element_type=f32)
           + jnp.dot(c_ref[0], wc_ref[...], preferred_element_type=f32))
    x1 = x_ref[0] + g1_ref[0] * mix
    x1_ref[0] = x1
    ms = jnp.mean(x1 * x1, axis=-1, keepdims=True)
    h = x1 * lax.rsqrt(ms + EPS) * nf_ref[...]
    h2 = h * (1.0 + sc_ref[0]) + sh_ref[0]
    hi = h2.astype(bf16)
    lo = (h2 - hi.astype(f32)).astype(bf16)
    h2_ref[0] = hi
    r1 = jnp.dot(hi, wr1_ref[...], preferred_element_type=f32)
    r2 = jnp.dot(lo, wr2_ref[...], preferred_element_type=f32)
    logits = r1[:, :ne] + r1[:, ne:] + r2
    m = jnp.max(logits, axis=-1, keepdims=True)
    e = jnp.exp(logits - m)
    aff_ref[0] = e / jnp.sum(e, axis=-1, keepdims=True)


def _outproj(attn, conv, x, gate1, scale2, shift2, norm_ffn, w_a, w_c, wr1, wr2):
    b, s, d = x.shape
    da = attn.shape[1]
    dc = conv.shape[2]
    ne = wr2.shape[1]
    tm = TOKEN_TILE
    const = lambda shape: pl.BlockSpec(shape, lambda bi, i: (0,) * len(shape))
    per_b = pl.BlockSpec((1, 1, d), lambda bi, i: (bi, 0, 0))
    return pl.pallas_call(
        functools.partial(_outproj_kernel, ne=ne),
        grid=(b, s // tm),
        in_specs=[pl.BlockSpec((1, da, tm), lambda bi, i: (bi, 0, i)),
                  pl.BlockSpec((1, tm, dc), lambda bi, i: (bi, i, 0)),
                  pl.BlockSpec((1, tm, d), lambda bi, i: (bi, i, 0)),
                  per_b, per_b, per_b, const((1, d)), const((da, d)), const((dc, d)),
                  const((d, 2 * ne)), const((d, ne))],
        out_specs=[pl.BlockSpec((1, tm, d), lambda bi, i: (bi, i, 0)),
                   pl.BlockSpec((1, tm, d), lambda bi, i: (bi, i, 0)),
                   pl.BlockSpec((1, tm, ne), lambda bi, i: (bi, i, 0))],
        out_shape=[jax.ShapeDtypeStruct((b, s, d), f32),
                   jax.ShapeDtypeStruct((b, s, d), bf16),
                   jax.ShapeDtypeStruct((b, s, ne), f32)],
        compiler_params=_params("arbitrary", "arbitrary"),
        name="outproj",
    )(attn, conv, x, gate1, scale2, shift2, norm_ffn, w_a, w_c, wr1, wr2)


def _thresh_kernel(bits_ref, thr_ref, ntie_ref, *, cap, ne, chunk):
    rows = bits_ref.shape[0]

    def count_ge(cand):
        cand_row = cand[0:1, :]

        def body(i, acc):
            blk = bits_ref[pl.ds(pl.multiple_of(i * chunk, chunk), chunk), :]
            hit = jnp.where(blk >= cand_row, 1, 0).astype(jnp.int32)
            return acc + jnp.sum(hit.reshape(chunk // 8, 8, LANES), axis=0)
        acc = lax.fori_loop(0, rows // chunk, body, jnp.zeros((8, LANES), jnp.int32))
        tot = jnp.sum(acc, axis=0, keepdims=True)
        tot = jnp.broadcast_to(tot, (8, LANES))
        shift = LANES // 2
        while shift >= ne:
            tot = tot + pltpu.roll(tot, shift, 1)
            shift //= 2
        return tot

    def bit_step(i, prefix):
        cand = prefix | jnp.left_shift(jnp.int32(1), 30 - i)
        return jnp.where(count_ge(cand) >= cap, cand, prefix)

    prefix = lax.fori_loop(0, 31, bit_step, jnp.zeros((8, LANES), jnp.int32))
    thr_ref[...] = prefix
    ntie_ref[...] = cap - count_ge(prefix + 1)


def _thresh(aff_bits, cap, ne):
    rows = aff_bits.shape[0]
    chunk = min(512, rows)
    return pl.pallas_call(
        functools.partial(_thresh_kernel, cap=cap, ne=ne, chunk=chunk),
        out_shape=[jax.ShapeDtypeStruct((8, LANES), jnp.int32)] * 2,
        compiler_params=pltpu.CompilerParams(vmem_limit_bytes=VMEM_LIMIT_BYTES),
        name="thresh",
    )(aff_bits)


def _plan_kernel(aff_ref, thr_ref, ntie_ref, ut_ref, slot_ref, cnt_ref, base_ref, used_ref, base_s, tie_s):
    @pl.when(pl.program_id(0) == 0)
    def _():
        base_s[...] = jnp.zeros_like(base_s)
        tie_s[...] = jnp.zeros_like(tie_s)

    a = aff_ref[...]
    thr = thr_ref[...]
    eq = a == thr
    eqf = jnp.where(eq, 1.0, 0.0)
    ut = ut_ref[...]
    tie_rank = tie_s[...] + jnp.dot(eqf.astype(bf16), ut, preferred_element_type=f32)
    sel = (a > thr) | (eq & (tie_rank < ntie_ref[...]))
    self_ = jnp.where(sel, 1.0, 0.0)
    rank = jnp.dot(self_.astype(bf16), ut, preferred_element_type=f32)
    cnt = jnp.sum(self_, axis=1, keepdims=True)
    base = base_s[...]
    slot_ref[...] = jnp.where(sel, base + rank, -1.0)
    cnt_ref[0] = cnt.astype(jnp.int32)
    base_ref[0] = base.astype(jnp.int32)
    new_base = base + jnp.ceil(cnt * (1.0 / BF16_SUBLANES)) * BF16_SUBLANES
    base_s[...] = new_base
    used_ref[...] = new_base.astype(jnp.int32)
    tie_s[...] = tie_s[...] + jnp.sum(eqf, axis=1, keepdims=True)


def _plan(aff_t, thr, ntie, ut):
    ne, n = aff_t.shape
    t = MOE_TILE
    tiles = n // t
    return pl.pallas_call(
        _plan_kernel,
        grid=(tiles,),
        in_specs=[pl.BlockSpec((ne, t), lambda i: (0, i)),
                  pl.BlockSpec((ne, 1), lambda i: (0, 0)),
                  pl.BlockSpec((ne, 1), lambda i: (0, 0)),
                  pl.BlockSpec((t, t), lambda i: (0, 0))],
        out_specs=[pl.BlockSpec((ne, t), lambda i: (0, i)),
                   pl.BlockSpec((1, ne, 1), lambda i: (i, 0, 0)),
                   pl.BlockSpec((1, ne, 1), lambda i: (i, 0, 0)),
                   pl.BlockSpec((ne, 1), lambda i: (0, 0))],
        out_shape=[jax.ShapeDtypeStruct((ne, n), f32),
                   jax.ShapeDtypeStruct((tiles, ne, 1), jnp.int32),
                   jax.ShapeDtypeStruct((tiles, ne, 1), jnp.int32),
                   jax.ShapeDtypeStruct((ne, 1), jnp.int32)],
        scratch_shapes=[pltpu.VMEM((ne, 1), f32), pltpu.VMEM((ne, 1), f32)],
        compiler_params=_params("arbitrary"),
        name="plan",
    )(aff_t, thr, ntie, ut)


def _blocks_needed(cnt_sm, tile, ne):
    mx = cnt_sm[tile * ne]
    for e in range(1, ne):
        mx = jnp.maximum(mx, cnt_sm[tile * ne + e])
    return (mx + SLOT_BLOCK - 1) // SLOT_BLOCK


def _gather_copy(stage, xe_hbm, sem, buf, e, start):
    r = SLOT_BLOCK
    return pltpu.make_async_copy(stage.at[buf, pl.ds(e * r, r), :], xe_hbm.at[e, pl.ds(start, r), :], sem.at[buf])


def _gather_kernel(base_sm, cnt_sm, used_sm, slot_ref, slot_nx, h_ref, xe_hbm, sel_s, stage, zeros_s, sem,
                   *, ne, group):
    j = pl.program_id(0)
    last = pl.num_programs(0) - 1
    n_tiles = 2 * pl.num_programs(0)
    r = SLOT_BLOCK
    t = MOE_TILE
    rows_alloc = xe_hbm.shape[1]

    def wait_buf(buf):
        for e in range(ne):
            _gather_copy(stage, xe_hbm, sem, buf, e, 0).wait()

    def start_buf(tile, b, buf):
        for e in range(ne):
            start = pl.multiple_of(base_sm[tile * ne + e] + b * r, BF16_SUBLANES)
            _gather_copy(stage, xe_hbm, sem, buf, e, start).start()

    def one_hot(tile, b, sref, col0, buf):
        row_iota = lax.broadcasted_iota(jnp.int32, (r, t), 0).astype(f32)
        for e in range(ne):
            start = (base_sm[tile * ne + e] + b * r).astype(f32)
            hit = sref[e:e + 1, pl.ds(col0, t)] == row_iota + start
            sel_s[buf, pl.ds(e * r, r), :] = jnp.where(hit, 1.0, 0.0).astype(bf16)

    def pick(row0, buf):
        h = h_ref[pl.ds(row0, t), :]
        for c in range(ne // group):
            rows = pl.ds(c * group * r, group * r)
            stage[buf, rows, :] = jnp.dot(sel_s[buf, rows, :], h, preferred_element_type=f32).astype(bf16)

    def extra_blocks(tile, row0):
        def extra(b, _):
            one_hot(tile, b, slot_ref, row0, 2)
            pick(row0, 2)
            start_buf(tile, b, 2)
            wait_buf(2)
            return 0

        lax.fori_loop(1, jnp.maximum(_blocks_needed(cnt_sm, tile, ne), 1), extra, 0)

    t0 = 2 * j
    t1 = t0 + 1
    t2 = jnp.minimum(t0 + 2, n_tiles - 1)

    @pl.when(j == 0)
    def _():
        one_hot(0, 0, slot_ref, 0, 0)

    pick(0, 0)
    one_hot(t1, 0, slot_ref, t, 1)

    @pl.when(j > 0)
    def _():
        wait_buf(1)

    start_buf(t0, 0, 0)
    extra_blocks(t0, 0)

    pick(t, 1)
    one_hot(t2, 0, slot_nx, 0, 0)
    wait_buf(0)
    start_buf(t1, 0, 1)
    extra_blocks(t1, t)

    @pl.when(j == last)
    def _():
        wait_buf(1)

    @pl.when(j == last)
    def _():
        zeros_s[...] = jnp.zeros_like(zeros_s)

        def fill_copy(e, start, rows):
            return pltpu.make_async_copy(zeros_s.at[pl.ds(0, rows), :], xe_hbm.at[e, pl.ds(start, rows), :],
                                         sem.at[3])

        small = BF16_SUBLANES
        for e in range(ne):
            used = used_sm[e]
            n_big = (rows_alloc - used) // r
            n_small = (rows_alloc - used - n_big * r) // small

            def fill_big(k, _, e=e, used=used):
                fill_copy(e, pl.multiple_of(used + k * r, small), r).start()
                return 0

            def fill_small(k, _, e=e, used=used, n_big=n_big):
                fill_copy(e, pl.multiple_of(used + n_big * r + k * small, small), small).start()
                return 0

            def drain_big(k, _, e=e):
                fill_copy(e, 0, r).wait()
                return 0

            def drain_small(k, _, e=e):
                fill_copy(e, 0, small).wait()
                return 0

            lax.fori_loop(0, n_big, fill_big, 0)
            lax.fori_loop(0, n_small, fill_small, 0)
            lax.fori_loop(0, n_big, drain_big, 0)
            lax.fori_loop(0, n_small, drain_small, 0)


def _gather(base, cnt, used, slot_t, h2, rows_alloc):
    ne, n = slot_t.shape
    d = h2.shape[1]
    t = MOE_TILE
    tiles = n // t
    group = 4
    return pl.pallas_call(
        functools.partial(_gather_kernel, ne=ne, group=group),
        grid_spec=pltpu.PrefetchScalarGridSpec(
            num_scalar_prefetch=3,
            grid=(tiles // 2,),
            in_specs=[pl.BlockSpec((ne, 2 * t), lambda j, *_: (0, j)),
                      pl.BlockSpec((ne, t), lambda j, *_: (0, jnp.minimum(2 * j + 2, tiles - 1))),
                      pl.BlockSpec((2 * t, d), lambda j, *_: (j, 0))],
            out_specs=pl.BlockSpec(memory_space=pl.ANY),
            scratch_shapes=[pltpu.VMEM((3, ne * SLOT_BLOCK, t), bf16),
                            pltpu.VMEM((3, ne * SLOT_BLOCK, d), bf16),
                            pltpu.VMEM((SLOT_BLOCK, d), bf16),
                            pltpu.SemaphoreType.DMA((4,))]),
        out_shape=jax.ShapeDtypeStruct((ne, rows_alloc, d), bf16),
        compiler_params=_params("arbitrary"),
        name="gather",
    )(base, cnt, used, slot_t, slot_t, h2)


def _ffn_kernel(used_sm, x_ref, wg_ref, wu_ref, wd_ref, o_ref):
    e = pl.program_id(0)
    tm = x_ref.shape[1]
    r0 = pl.program_id(1) * tm
    used = used_sm[e]

    @pl.when(r0 < used)
    def _():
        x = x_ref[0]
        a = jnp.dot(x, wg_ref[0], preferred_element_type=f32)
        b = jnp.dot(x, wu_ref[0], preferred_element_type=f32)
        hid = (_silu(a) * b).astype(bf16)
        o_ref[0] = jnp.dot(hid, wd_ref[0], preferred_element_type=f32).astype(bf16)

    @pl.when(r0 >= used)
    def _():
        o_ref[...] = jnp.zeros_like(o_ref)


def _ffn(used, xe, wg, wu, wd):
    ne, rows, d = xe.shape
    f = wg.shape[2]
    tm = FFN_TILE
    return pl.pallas_call(
        _ffn_kernel,
        grid_spec=pltpu.PrefetchScalarGridSpec(
            num_scalar_prefetch=1,
            grid=(ne, rows // tm),
            in_specs=[pl.BlockSpec((1, tm, d), lambda e, i, *_: (e, i, 0)),
                      pl.BlockSpec((1, d, f), lambda e, i, *_: (e, 0, 0)),
                      pl.BlockSpec((1, d, f), lambda e, i, *_: (e, 0, 0)),
                      pl.BlockSpec((1, f, d), lambda e, i, *_: (e, 0, 0))],
            out_specs=pl.BlockSpec((1, tm, d), lambda e, i, *_: (e, i, 0))),
        out_shape=jax.ShapeDtypeStruct((ne, rows, d), bf16),
        compiler_params=_params("arbitrary", "arbitrary"),
        name="ffn",
    )(used, xe, wg, wu, wd)


def _combine_copy(ye_hbm, ybuf, sem, buf, e, start):
    r = SLOT_BLOCK
    return pltpu.make_async_copy(ye_hbm.at[e, pl.ds(start, r), :], ybuf.at[buf, pl.ds(e * r, r), :], sem.at[buf])


def _combine_kernel(base_sm, cnt_sm, slot_ref, aff_ref, slot_nx, aff_nx, x1_ref, g2_ref, ye_hbm, o_ref,
                    sel_s, ybuf, sem, *, ne):
    j = pl.program_id(0)
    last = pl.num_programs(0) - 1
    r = SLOT_BLOCK
    t = MOE_TILE
    n_tiles = 2 * pl.num_programs(0)

    def start_block(tile, b, buf):
        for e in range(ne):
            start = pl.multiple_of(base_sm[tile * ne + e] + b * r, BF16_SUBLANES)
            _combine_copy(ye_hbm, ybuf, sem, buf, e, start).start()

    def wait_block(buf):
        for e in range(ne):
            _combine_copy(ye_hbm, ybuf, sem, buf, e, 0).wait()

    def one_hot(tile, b, sref, aref, row0, buf):
        lane_iota = lax.broadcasted_iota(jnp.int32, (t, r), 1).astype(f32)
        rows = pl.ds(row0, t)
        for e in range(ne):
            start = (base_sm[tile * ne + e] + b * r).astype(f32)
            hit = sref[rows, e:e + 1] == lane_iota + start
            sel_s[buf, :, pl.ds(e * r, r)] = jnp.where(hit, aref[rows, e:e + 1], 0.0).astype(bf16)

    def finish(row0, buf):
        rows = pl.ds(row0, t)
        o_ref[rows, :] = x1_ref[rows, :] + g2_ref[0] * jnp.dot(sel_s[buf], ybuf[buf], preferred_element_type=f32)

    def extra_blocks(tile, row0):
        rows = pl.ds(row0, t)

        def extra(b, _):
            start_block(tile, b, 2)
            one_hot(tile, b, slot_ref, aff_ref, row0, 2)
            wait_block(2)
            o_ref[rows, :] += g2_ref[0] * jnp.dot(sel_s[2], ybuf[2], preferred_element_type=f32)
            return 0

        lax.fori_loop(1, jnp.maximum(_blocks_needed(cnt_sm, tile, ne), 1), extra, 0)

    t0 = 2 * j
    t1 = t0 + 1
    t2 = jnp.minimum(t0 + 2, n_tiles - 1)

    @pl.when(j == 0)
    def _():
        one_hot(0, 0, slot_ref, aff_ref, 0, 0)
        start_block(0, 0, 0)

    start_block(t1, 0, 1)
    wait_block(0)
    finish(0, 0)
    one_hot(t1, 0, slot_ref, aff_ref, t, 1)
    extra_blocks(t0, 0)

    @pl.when(j < last)
    def _():
        start_block(t2, 0, 0)

    wait_block(1)
    finish(t, 1)
    one_hot(t2, 0, slot_nx, aff_nx, 0, 0)
    extra_blocks(t1, t)


def _combine(base, cnt, slot, aff, x1, gate2, ye, seq):
    n, ne = slot.shape
    d = x1.shape[1]
    t = MOE_TILE
    tiles = n // t
    per_seq = seq // (2 * t)
    nxt = lambda j, *_: (jnp.minimum(2 * j + 2, tiles - 1), 0)
    return pl.pallas_call(
        functools.partial(_combine_kernel, ne=ne),
        grid_spec=pltpu.PrefetchScalarGridSpec(
            num_scalar_prefetch=2,
            grid=(tiles // 2,),
            in_specs=[pl.BlockSpec((2 * t, ne), lambda j, *_: (j, 0)),
                      pl.BlockSpec((2 * t, ne), lambda j, *_: (j, 0)),
                      pl.BlockSpec((t, ne), nxt),
                      pl.BlockSpec((t, ne), nxt),
                      pl.BlockSpec((2 * t, d), lambda j, *_: (j, 0)),
                      pl.BlockSpec((1, 1, d), lambda j, *_: (j // per_seq, 0, 0)),
                      pl.BlockSpec(memory_space=pl.ANY)],
            out_specs=pl.BlockSpec((2 * t, d), lambda j, *_: (j, 0)),
            scratch_shapes=[pltpu.VMEM((3, t, ne * SLOT_BLOCK), bf16),
                            pltpu.VMEM((3, ne * SLOT_BLOCK, d), bf16),
                            pltpu.SemaphoreType.DMA((3,))]),
        out_shape=jax.ShapeDtypeStruct((n, d), f32),
        compiler_params=_params("arbitrary"),
        name="combine",
    )(base, cnt, slot, aff, slot, aff, x1, gate2, ye)


def _rope_tables(s):
    t = jnp.arange(s)
    row = (t // GRID_W).astype(f32)
    col = (t % GRID_W).astype(f32)
    axis_dim = HEAD_DIM // 2
    inv = ROPE_THETA ** (-jnp.arange(0, axis_dim, 2, dtype=f32) / axis_dim)
    ang = jnp.concatenate([row[:, None] * inv, col[:, None] * inv], axis=-1)
    cos, sin = jnp.cos(ang), jnp.sin(ang)
    reps = LANES // HEAD_DIM
    cos_t = jnp.tile(jnp.concatenate([cos, cos], axis=-1), (1, reps))
    sin_t = jnp.tile(jnp.concatenate([-sin, sin], axis=-1), (1, reps))
    return cos_t, sin_t


def _head_perm(n_heads):
    half = jnp.concatenate([jnp.arange(0, HEAD_DIM, 2), jnp.arange(1, HEAD_DIM, 2)])
    return (jnp.arange(n_heads)[:, None] * HEAD_DIM + half[None, :]).reshape(-1)


def _block_mean(width):
    idx = jnp.arange(width) // HEAD_DIM
    return jnp.where(idx[:, None] == idx[None, :], 1.0 / HEAD_DIM, 0.0).astype(bf16)


def _prepare(norm_mix, w_in, q_norm, k_norm, conv_w, conv_b, conv_ln_g, conv_ln_b, w_out, norm_ffn,
             w_router, w_gate, w_up, w_down):
    d = w_in.shape[0]
    d_conv = conv_w.shape[-1]
    d_attn = w_out.shape[0] - d_conv
    d_kv = (w_in.shape[1] - d_attn - 2 * d_conv) // 2
    n_heads = d_attn // HEAD_DIM
    n_kv = d_kv // HEAD_DIM
    qp = _head_perm(n_heads)
    kp = _head_perm(n_kv)
    cols = jnp.concatenate([qp, d_attn + kp, jnp.arange(d_attn + d_kv, w_in.shape[1])])
    wr_hi = w_router.astype(bf16)
    wr_lo = (w_router - wr_hi.astype(f32)).astype(bf16)
    return dict(
        d_attn=d_attn, d_kv=d_kv,
        norm_mix=norm_mix.reshape(1, d),
        w_in=w_in[:, cols].astype(bf16),
        qg=jnp.tile(q_norm[_head_perm(1)], n_heads).reshape(1, d_attn),
        kg=jnp.tile(k_norm[_head_perm(1)], n_kv).reshape(1, d_kv),
        bdq=_block_mean(d_attn), bdk=_block_mean(d_kv),
        conv_w=conv_w.reshape(conv_w.shape[0], d_conv), conv_b=conv_b.reshape(1, d_conv),
        conv_ln_g=conv_ln_g.reshape(1, d_conv), conv_ln_b=conv_ln_b.reshape(1, d_conv),
        w_a=w_out[:d_attn].astype(bf16), w_c=w_out[d_attn:].astype(bf16),
        norm_ffn=norm_ffn.reshape(1, d),
        wr1=jnp.concatenate([wr_hi, wr_lo], axis=1), wr2=wr_hi,
        w_gate=w_gate.astype(bf16), w_up=w_up.astype(bf16), w_down=w_down.astype(bf16),
    )


def _encoder_layer(x, mod, p):
    b, s, d = x.shape
    shift1, scale1, gate1, shift2, scale2, gate2 = [m.reshape(b, 1, d) for m in jnp.split(mod, 6, axis=-1)]
    cos, sin = _rope_tables(s)
    q, k, v, u = _inproj(x, shift1, scale1, p["norm_mix"], p["w_in"], p["qg"], p["kg"], p["bdq"], p["bdk"],
                         cos, sin, p["d_attn"], p["d_kv"])
    attn = _attention(q, k, v)
    conv = _conv(u, p["conv_w"], p["conv_b"], p["conv_ln_g"], p["conv_ln_b"])
    x1, h2, aff = _outproj(attn, conv, x, gate1, scale2, shift2, p["norm_ffn"], p["w_a"], p["w_c"],
                           p["wr1"], p["wr2"])

    n = b * s
    ne = aff.shape[-1]
    cap = CAPACITY_FACTOR * n // ne
    aff = aff.reshape(n, ne)
    bits = lax.bitcast_convert_type(aff, jnp.int32).reshape(n * ne // LANES, LANES)
    thr_bits, ntie = _thresh(bits, cap, ne)
    thr = lax.bitcast_convert_type(thr_bits[0, :ne], f32).reshape(ne, 1)
    ntie = ntie[0, :ne].astype(f32).reshape(ne, 1)
    t = MOE_TILE
    tok = jnp.arange(t)
    ut = (tok[:, None] < tok[None, :]).astype(bf16)
    slot_t, cnt, base, used = _plan(aff.T, thr, ntie, ut)
    cnt = cnt.reshape(-1)
    base = base.reshape(-1)
    tiles = n // t
    rows_alloc = -(-(cap + BF16_SUBLANES * tiles + SLOT_BLOCK) // FFN_TILE) * FFN_TILE
    used = used.reshape(-1)
    xe = _gather(base, cnt, used, slot_t, h2.reshape(n, d), rows_alloc)
    ye = _ffn(used, xe, p["w_gate"], p["w_up"], p["w_down"])
    out = _combine(base, cnt, slot_t.T, aff, x1.reshape(n, d), gate2, ye, s)
    return out.reshape(b, s, d)


def kernel(x_prompt, x_sample, c_prompt, c_sample, ada_w, ada_b, norm_mix, w_in, q_norm, k_norm, conv_w, conv_b,
           conv_ln_g, conv_ln_b, w_out, norm_ffn, w_router, w_gate, w_up, w_down):
    y_prompt, y_sample = x_prompt, x_sample
    nb = x_prompt.shape[0]
    for l in range(ada_w.shape[0]):
        p = _prepare(norm_mix[l], w_in[l], q_norm[l], k_norm[l], conv_w[l], conv_b[l], conv_ln_g[l],
                     conv_ln_b[l], w_out[l], norm_ffn[l], w_router[l], w_gate[l], w_up[l], w_down[l])
        mod = _ada(jnp.concatenate([c_prompt, c_sample], axis=0), ada_w[l], ada_b[l])
        y_prompt = _encoder_layer(y_prompt, mod[:nb], p)
        y_sample = _encoder_layer(y_sample, mod[nb:], p)
    return (y_prompt, y_sample)
```

```python
import functools

import jax
import jax.numpy as jnp
from jax import lax
from jax.experimental import pallas as pl
from jax.experimental.pallas import tpu as pltpu

HEAD_DIM = 64
N_KV_HEADS = 2
GRID_W = 64
ROPE_THETA = 10000.0
CAPACITY_FACTOR = 2
EPS = 1e-6
LOG2E = 1.4426950408889634

LANES = 128
BF16_SUBLANES = 16
VMEM_LIMIT_BYTES = 48 * 1024 * 1024

TOKEN_TILE = 512
MOE_TILE = 512
SLOT_BLOCK = 96
SLOT_LANES = LANES
FFN_TILE = 512
Q_TILE = 256
K_TILE = 256
ATTN_STRIP = 256
SCORE_BOUND_LOG2 = 56.0
ATTN_BLOCKS_PER_TRIP = 8
CONV_TILE = 512
CONV_HALO = 16
CONV_ROWS = 64

f32 = jnp.float32
bf16 = jnp.bfloat16


def _params(*sem):
    return pltpu.CompilerParams(dimension_semantics=sem, vmem_limit_bytes=VMEM_LIMIT_BYTES)


def _silu(x):
    return x * jax.nn.sigmoid(x)


def _ada_kernel(c_ref, w_ref, b_ref, o_ref):
    c = c_ref[...]
    o_ref[...] = jnp.dot(_silu(c), w_ref[...], preferred_element_type=f32,
                         precision=lax.Precision.HIGHEST) + b_ref[...]


def _ada(c, w, b):
    r, d = c.shape
    n = w.shape[1]
    tn = n // 4
    return pl.pallas_call(
        _ada_kernel,
        grid=(n // tn,),
        in_specs=[pl.BlockSpec((r, d), lambda j: (0, 0)),
                  pl.BlockSpec((d, tn), lambda j: (0, j)),
                  pl.BlockSpec((1, tn), lambda j: (0, j))],
        out_specs=pl.BlockSpec((r, tn), lambda j: (0, j)),
        out_shape=jax.ShapeDtypeStruct((r, n), f32),
        compiler_params=_params("arbitrary"),
        name="ada",
    )(c, w, b.reshape(1, n))


def _norm_rope(t, bd_ref, gain_ref, cos, sin, scale):
    ms = jnp.dot((t * t).astype(bf16), bd_ref[...], preferred_element_type=f32)
    tn = t * lax.rsqrt(ms + EPS) * gain_ref[...]
    w = t.shape[1]
    reps = w // LANES
    if reps > 1:
        cos = jnp.concatenate([cos] * reps, axis=1)
        sin = jnp.concatenate([sin] * reps, axis=1)
    lane = lax.broadcasted_iota(jnp.int32, tn.shape, 1)
    first_half = (lane % HEAD_DIM) < (HEAD_DIM // 2)
    partner = jnp.where(first_half, pltpu.roll(tn, w - HEAD_DIM // 2, 1), pltpu.roll(tn, HEAD_DIM // 2, 1))
    return (tn * cos + partner * sin) * scale


def _inproj_kernel(x_ref, sh_ref, sc_ref, g_ref, w_ref, qg_ref, kg_ref, bdq_ref, bdk_ref, cos_ref, sin_ref,
                   q_ref, k_ref, v_ref, u_ref, *, d_attn, d_kv):
    x = x_ref[0]
    ms = jnp.mean(x * x, axis=-1, keepdims=True)
    h = x * lax.rsqrt(ms + EPS) * g_ref[...]
    h = h * (1.0 + sc_ref[0]) + sh_ref[0]
    z = jnp.dot(h.astype(bf16), w_ref[...], preferred_element_type=f32)
    cos = cos_ref[...]
    sin = sin_ref[...]
    q = _norm_rope(z[:, :d_attn], bdq_ref, qg_ref, cos, sin, HEAD_DIM ** -0.5 * LOG2E)
    k = _norm_rope(z[:, d_attn:d_attn + d_kv], bdk_ref, kg_ref, cos, sin, 1.0)
    v = z[:, d_attn + d_kv:d_attn + 2 * d_kv]
    q_ref[0] = q.T.astype(bf16)
    vt = v.T.astype(bf16)
    for g in range(d_kv // HEAD_DIM):
        k_ref[0, g] = k[:, g * HEAD_DIM:(g + 1) * HEAD_DIM].astype(bf16)
        for j in range(v_ref.shape[2]):
            v_ref[0, g, j] = vt[g * HEAD_DIM:(g + 1) * HEAD_DIM, j * K_TILE:(j + 1) * K_TILE]
    u_ref[0] = z[:, d_attn + 2 * d_kv:].astype(bf16)


def _inproj(x, shift, scale, gain, w_in, qg, kg, bdq, bdk, cos, sin, d_attn, d_kv):
    b, s, d = x.shape
    d_in = w_in.shape[1]
    d_u = d_in - d_attn - 2 * d_kv
    n_kv = d_kv // HEAD_DIM
    tm = TOKEN_TILE
    const = lambda shape: pl.BlockSpec(shape, lambda bi, i: (0,) * len(shape))
    per_b = pl.BlockSpec((1, 1, d), lambda bi, i: (bi, 0, 0))
    return pl.pallas_call(
        functools.partial(_inproj_kernel, d_attn=d_attn, d_kv=d_kv),
        grid=(b, s // tm),
        in_specs=[pl.BlockSpec((1, tm, d), lambda bi, i: (bi, i, 0)), per_b, per_b,
                  const((1, d)), const((d, d_in)), const((1, d_attn)), const((1, d_kv)),
                  const((d_attn, d_attn)), const((d_kv, d_kv)),
                  pl.BlockSpec((tm, LANES), lambda bi, i: (i, 0)),
                  pl.BlockSpec((tm, LANES), lambda bi, i: (i, 0))],
        out_specs=[pl.BlockSpec((1, d_attn, tm), lambda bi, i: (bi, 0, i)),
                   pl.BlockSpec((1, n_kv, tm, HEAD_DIM), lambda bi, i: (bi, 0, i, 0)),
                   pl.BlockSpec((1, n_kv, tm // K_TILE, HEAD_DIM, K_TILE), lambda bi, i: (bi, 0, i, 0, 0)),
                   pl.BlockSpec((1, tm, d_u), lambda bi, i: (bi, i, 0))],
        out_shape=[jax.ShapeDtypeStruct((b, d_attn, s), bf16),
                   jax.ShapeDtypeStruct((b, n_kv, s, HEAD_DIM), bf16),
                   jax.ShapeDtypeStruct((b, n_kv, s // K_TILE, HEAD_DIM, K_TILE), bf16),
                   jax.ShapeDtypeStruct((b, s, d_u), bf16)],
        compiler_params=_params("arbitrary", "arbitrary"),
        name="inproj",
    )(x, shift, scale, gain, w_in, qg, kg, bdq, bdk, cos, sin)


def _attn_kernel(q_ref, k_ref, v_ref, o_ref, s_scr, q_scr, m_scr, acc_scr, *, n_rep, bounded):
    tq = q_ref.shape[2]
    n_blk, _, tk = v_ref.shape[2:]
    qt = q_ref[0]
    qs = jnp.concatenate([qt[h * HEAD_DIM:(h + 1) * HEAD_DIM, :] for h in range(n_rep)], axis=1)
    cols = qs.shape[1]
    ones = jnp.ones((BF16_SUBLANES, tk), bf16)

    strips = [pl.ds(c, ATTN_STRIP) for c in range(0, cols, ATTN_STRIP)]

    def scores(i, slot):
        kt = k_ref[0, 0, pl.ds(pl.multiple_of(jnp.minimum(i, n_blk - 1) * tk, tk), tk), :]
        for c in strips:
            s_scr[slot, :, c] = jnp.dot(kt, q_scr[:, c], preferred_element_type=f32)

    def attend(i, slot):
        vt = jnp.concatenate([v_ref[0, 0, i], ones], axis=0)
        for c in strips:
            if bounded:
                p = jnp.exp2(s_scr[slot, :, c]).astype(bf16)
                acc_scr[:, c] += jnp.dot(vt, p, preferred_element_type=f32)
            else:
                m = m_scr[:, c]
                m_new = jnp.maximum(m, jnp.max(s_scr[slot, :, c], axis=0, keepdims=True))
                p = jnp.exp2(s_scr[slot, :, c] - m_new).astype(bf16)
                acc_scr[:, c] = jnp.exp2(m - m_new) * acc_scr[:, c] + jnp.dot(vt, p, preferred_element_type=f32)
                m_scr[:, c] = m_new

    per_trip = min(ATTN_BLOCKS_PER_TRIP, n_blk)

    def body(j, _):
        for step in range(per_trip):
            i = j * per_trip + step
            slot = step % 2
            scores(i + 1, 1 - slot)
            attend(i, slot)
        return 0

    q_scr[...] = qs
    m_scr[...] = jnp.full(m_scr.shape, -jnp.inf, f32)
    acc_scr[...] = jnp.zeros(acc_scr.shape, f32)
    scores(0, 0)
    lax.fori_loop(0, n_blk // per_trip, body, 0)
    acc = acc_scr[...]
    o = acc[:HEAD_DIM] / acc[HEAD_DIM:HEAD_DIM + 1]
    o_ref[0] = jnp.concatenate([o[:, h * tq:(h + 1) * tq] for h in range(n_rep)], axis=0).astype(bf16)


def _attention(qt, k, vt, bounded):
    b, d_attn, s = qt.shape
    n_kv, n_blk, _, tk = vt.shape[1:]
    n_rep = d_attn // HEAD_DIM // n_kv
    tq = min(Q_TILE, s)
    gw = n_rep * HEAD_DIM
    return pl.pallas_call(
        functools.partial(_attn_kernel, n_rep=n_rep, bounded=bounded),
        grid=(b, n_kv, s // tq),
        in_specs=[pl.BlockSpec((1, gw, tq), lambda bi, g, i: (bi, g, i)),
                  pl.BlockSpec((1, 1, s, HEAD_DIM), lambda bi, g, i: (bi, g, 0, 0)),
                  pl.BlockSpec((1, 1, n_blk, HEAD_DIM, tk), lambda bi, g, i: (bi, g, 0, 0, 0))],
        out_specs=pl.BlockSpec((1, gw, tq), lambda bi, g, i: (bi, g, i)),
        out_shape=jax.ShapeDtypeStruct((b, d_attn, s), bf16),
        scratch_shapes=[pltpu.VMEM((2, tk, n_rep * tq), f32),
                        pltpu.VMEM((HEAD_DIM, n_rep * tq), bf16), pltpu.VMEM((1, n_rep * tq), f32),
                        pltpu.VMEM((HEAD_DIM + BF16_SUBLANES, n_rep * tq), f32)],
        compiler_params=_params("arbitrary", "arbitrary", "arbitrary"),
        name="attn_bounded" if bounded else "attn",
    )(qt, k, vt)


def _conv_kernel(u_ref, up_ref, un_ref, w_ref, cb_ref, lg_ref, lb_ref, o_ref, a_ref, y_ref, *, ts, dc, width):
    i = pl.program_id(1)
    last = pl.num_programs(1) - 1
    halo = CONV_HALO

    def glu(u):
        u = u.astype(f32)
        return u[:, :dc] * jax.nn.sigmoid(u[:, dc:])

    a_ref[pl.ds(halo, ts), :] = glu(u_ref[0])
    a_ref[pl.ds(0, halo), :] = jnp.where(i > 0, glu(up_ref[0]), 0.0)
    a_ref[pl.ds(halo + ts, halo), :] = jnp.where(i < last, glu(un_ref[0]), 0.0)

    off = halo - width // 2
    sub = 8
    rows = CONV_ROWS
    reach = -(-(off + width) // sub) * sub
    cb = cb_ref[...]
    lg = lg_ref[...]
    lb = lb_ref[...]
    for c in range(ts // rows):
        r0 = c * rows
        for g in range(dc // LANES):
            lanes = pl.ds(g * LANES, LANES)
            blk = a_ref[pl.ds(r0, rows + reach), lanes]
            acc = None
            for r in range(sub):
                part = None
                for u in range(r, off + width, sub):
                    if u < off:
                        continue
                    term = blk[u - r:u - r + rows + sub, :] * w_ref[pl.ds(u - off, 1), lanes]
                    part = term if part is None else part + term
                shifted = part[r:r + rows, :]
                acc = shifted if acc is None else acc + shifted
            y_ref[pl.ds(r0, rows), lanes] = acc
        y = y_ref[pl.ds(r0, rows), :] + cb
        mu = jnp.mean(y, axis=-1, keepdims=True)
        yc = y - mu
        var = jnp.mean(yc * yc, axis=-1, keepdims=True)
        yn = yc * lax.rsqrt(var + EPS) * lg + lb
        o_ref[0, pl.ds(r0, CONV_ROWS), :] = _silu(yn).astype(bf16)


def _conv(u, w, cb, lg, lb):
    b, s, du = u.shape
    dc = du // 2
    width = w.shape[0]
    ts = min(CONV_TILE, s)
    hb = ts // CONV_HALO
    n_h = s // CONV_HALO
    const = lambda shape: pl.BlockSpec(shape, lambda bi, i: (0,) * len(shape))
    return pl.pallas_call(
        functools.partial(_conv_kernel, ts=ts, dc=dc, width=width),
        grid=(b, s // ts),
        in_specs=[pl.BlockSpec((1, ts, du), lambda bi, i: (bi, i, 0)),
                  pl.BlockSpec((1, CONV_HALO, du), lambda bi, i: (bi, jnp.maximum(i * hb - 1, 0), 0)),
                  pl.BlockSpec((1, CONV_HALO, du), lambda bi, i: (bi, jnp.minimum((i + 1) * hb, n_h - 1), 0)),
                  const((width, dc)), const((1, dc)), const((1, dc)), const((1, dc))],
        out_specs=pl.BlockSpec((1, ts, dc), lambda bi, i: (bi, i, 0)),
        out_shape=jax.ShapeDtypeStruct((b, s, dc), bf16),
        scratch_shapes=[pltpu.VMEM((ts + 2 * CONV_HALO, dc), f32), pltpu.VMEM((ts, dc), f32)],
        compiler_params=_params("arbitrary", "arbitrary"),
        name="conv",
    )(u, u, u, w, cb, lg, lb)


def _outproj_kernel(a_ref, c_ref, x_ref, g1_ref, sc_ref, sh_ref, nf_ref, wa_ref, wc_ref, wr1_ref, wr2_ref,
                    x1_ref, h2_ref, aff_ref, *, ne):
    attn = a_ref[0].astype(f32).T.astype(bf16)
    mix = (jnp.dot(attn, wa_ref[...], preferred_element_type=f32)
           + jnp.dot(c_ref[0], wc_ref[...], preferred_element_type=f32))
    x1 = x_ref[0] + g1_ref[0] * mix
    x1_ref[0] = x1
    ms = jnp.mean(x1 * x1, axis=-1, keepdims=True)
    h = x1 * lax.rsqrt(ms + EPS) * nf_ref[...]
    h2 = h * (1.0 + sc_ref[0]) + sh_ref[0]
    hi = h2.astype(bf16)
    lo = (h2 - hi.astype(f32)).astype(bf16)
    h2_ref[0] = hi
    r1 = jnp.dot(hi, wr1_ref[...], preferred_element_type=f32)
    r2 = jnp.dot(lo, wr2_ref[...], preferred_element_type=f32)
    logits = r1[:, :ne] + r1[:, ne:] + r2
    m = jnp.max(logits, axis=-1, keepdims=True)
    e = jnp.exp(logits - m)
    aff_ref[0] = e / jnp.sum(e, axis=-1, keepdims=True)


def _outproj(attn, conv, x, gate1, scale2, shift2, norm_ffn, w_a, w_c, wr1, wr2):
    b, s, d = x.shape
    da = attn.shape[1]
    dc = conv.shape[2]
    ne = wr2.shape[1]
    tm = TOKEN_TILE
    const = lambda shape: pl.BlockSpec(shape, lambda bi, i: (0,) * len(shape))
    per_b = pl.BlockSpec((1, 1, d), lambda bi, i: (bi, 0, 0))
    return pl.pallas_call(
        functools.partial(_outproj_kernel, ne=ne),
        grid=(b, s // tm),
        in_specs=[pl.BlockSpec((1, da, tm), lambda bi, i: (bi, 0, i)),
                  pl.BlockSpec((1, tm, dc), lambda bi, i: (bi, i, 0)),
                  pl.BlockSpec((1, tm, d), lambda bi, i: (bi, i, 0)),
                  per_b, per_b, per_b, const((1, d)), const((da, d)), const((dc, d)),
                  const((d, 2 * ne)), const((d, ne))],
        out_specs=[pl.BlockSpec((1, tm, d), lambda bi, i: (bi, i, 0)),
                   pl.BlockSpec((1, tm, d), lambda bi, i: (bi, i, 0)),
                   pl.BlockSpec((1, tm, ne), lambda bi, i: (bi, i, 0))],
        out_shape=[jax.ShapeDtypeStruct((b, s, d), f32),
                   jax.ShapeDtypeStruct((b, s, d), bf16),
                   jax.ShapeDtypeStruct((b, s, ne), f32)],
        compiler_params=_params("arbitrary", "arbitrary"),
        name="outproj",
    )(attn, conv, x, gate1, scale2, shift2, norm_ffn, w_a, w_c, wr1, wr2)


def _thresh_kernel(bits_ref, thr_ref, ntie_ref, *, cap, ne, chunk):
    rows = bits_ref.shape[0]

    def count_ge(cand):
        cand_row = cand[0:1, :]

        def body(i, acc):
            blk = bits_ref[pl.ds(pl.multiple_of(i * chunk, chunk), chunk), :]
            hit = jnp.where(blk >= cand_row, 1, 0).astype(jnp.int32)
            return acc + jnp.sum(hit.reshape(chunk // 8, 8, LANES), axis=0)
        acc = lax.fori_loop(0, rows // chunk, body, jnp.zeros((8, LANES), jnp.int32))
        tot = jnp.sum(acc, axis=0, keepdims=True)
        tot = jnp.broadcast_to(tot, (8, LANES))
        shift = LANES // 2
        while shift >= ne:
            tot = tot + pltpu.roll(tot, shift, 1)
            shift //= 2
        return tot

    def bit_step(i, prefix):
        cand = prefix | jnp.left_shift(jnp.int32(1), 30 - i)
        return jnp.where(count_ge(cand) >= cap, cand, prefix)

    prefix = lax.fori_loop(0, 31, bit_step, jnp.zeros((8, LANES), jnp.int32))
    thr_ref[...] = prefix
    ntie_ref[...] = cap - count_ge(prefix + 1)


def _thresh(aff_bits, cap, ne):
    rows = aff_bits.shape[0]
    chunk = min(512, rows)
    return pl.pallas_call(
        functools.partial(_thresh_kernel, cap=cap, ne=ne, chunk=chunk),
        out_shape=[jax.ShapeDtypeStruct((8, LANES), jnp.int32)] * 2,
        compiler_params=pltpu.CompilerParams(vmem_limit_bytes=VMEM_LIMIT_BYTES),
        name="thresh",
    )(aff_bits)


def _plan_kernel(aff_ref, thr_ref, ntie_ref, ut_ref, slot_ref, cnt_ref, base_ref, used_ref, base_s, tie_s):
    @pl.when(pl.program_id(0) == 0)
    def _():
        base_s[...] = jnp.zeros_like(base_s)
        tie_s[...] = jnp.zeros_like(tie_s)

    a = aff_ref[...]
    thr = thr_ref[...]
    eq = a == thr
    eqf = jnp.where(eq, 1.0, 0.0)
    ut = ut_ref[...]
    tie_rank = tie_s[...] + jnp.dot(eqf.astype(bf16), ut, preferred_element_type=f32)
    sel = (a > thr) | (eq & (tie_rank < ntie_ref[...]))
    self_ = jnp.where(sel, 1.0, 0.0)
    rank = jnp.dot(self_.astype(bf16), ut, preferred_element_type=f32)
    cnt = jnp.sum(self_, axis=1, keepdims=True)
    base = base_s[...]
    slot_ref[...] = jnp.where(sel, base + rank, -1.0)
    cnt_ref[0] = cnt.astype(jnp.int32)
    base_ref[0] = base.astype(jnp.int32)
    new_base = base + jnp.ceil(cnt * (1.0 / BF16_SUBLANES)) * BF16_SUBLANES
    base_s[...] = new_base
    used_ref[...] = new_base.astype(jnp.int32)
    tie_s[...] = tie_s[...] + jnp.sum(eqf, axis=1, keepdims=True)


def _plan(aff_t, thr, ntie, ut):
    ne, n = aff_t.shape
    t = MOE_TILE
    tiles = n // t
    return pl.pallas_call(
        _plan_kernel,
        grid=(tiles,),
        in_specs=[pl.BlockSpec((ne, t), lambda i: (0, i)),
                  pl.BlockSpec((ne, 1), lambda i: (0, 0)),
                  pl.BlockSpec((ne, 1), lambda i: (0, 0)),
                  pl.BlockSpec((t, t), lambda i: (0, 0))],
        out_specs=[pl.BlockSpec((ne, t), lambda i: (0, i)),
                   pl.BlockSpec((1, ne, 1), lambda i: (i, 0, 0)),
                   pl.BlockSpec((1, ne, 1), lambda i: (i, 0, 0)),
                   pl.BlockSpec((ne, 1), lambda i: (0, 0))],
        out_shape=[jax.ShapeDtypeStruct((ne, n), f32),
                   jax.ShapeDtypeStruct((tiles, ne, 1), jnp.int32),
                   jax.ShapeDtypeStruct((tiles, ne, 1), jnp.int32),
                   jax.ShapeDtypeStruct((ne, 1), jnp.int32)],
        scratch_shapes=[pltpu.VMEM((ne, 1), f32), pltpu.VMEM((ne, 1), f32)],
        compiler_params=_params("arbitrary"),
        name="plan",
    )(aff_t, thr, ntie, ut)


def _blocks_needed(cnt_sm, tile, ne):
    mx = cnt_sm[tile * ne]
    for e in range(1, ne):
        mx = jnp.maximum(mx, cnt_sm[tile * ne + e])
    return (mx + SLOT_BLOCK - 1) // SLOT_BLOCK


def _gather_copy(stage, xe_hbm, sem, buf, e, start):
    r = SLOT_BLOCK
    return pltpu.make_async_copy(stage.at[buf, pl.ds(e * r, r), :], xe_hbm.at[e, pl.ds(start, r), :], sem.at[buf])


def _gather_kernel(base_sm, cnt_sm, used_sm, slot_ref, slot_nx, h_ref, xe_hbm, sel_s, stage, zeros_s, sem,
                   *, ne, group):
    j = pl.program_id(0)
    last = pl.num_programs(0) - 1
    n_tiles = 2 * pl.num_programs(0)
    r = SLOT_BLOCK
    t = MOE_TILE
    rows_alloc = xe_hbm.shape[1]

    def wait_buf(buf):
        for e in range(ne):
            _gather_copy(stage, xe_hbm, sem, buf, e, 0).wait()

    def start_buf(tile, b, buf):
        for e in range(ne):
            start = pl.multiple_of(base_sm[tile * ne + e] + b * r, BF16_SUBLANES)
            _gather_copy(stage, xe_hbm, sem, buf, e, start).start()

    def one_hot(tile, b, sref, col0, buf):
        row_iota = lax.broadcasted_iota(jnp.int32, (r, t), 0).astype(f32)
        for e in range(ne):
            start = (base_sm[tile * ne + e] + b * r).astype(f32)
            hit = sref[e:e + 1, pl.ds(col0, t)] == row_iota + start
            sel_s[buf, pl.ds(e * r, r), :] = jnp.where(hit, 1.0, 0.0).astype(bf16)

    def pick(row0, buf):
        h = h_ref[pl.ds(row0, t), :]
        for c in range(ne // group):
            rows = pl.ds(c * group * r, group * r)
            stage[buf, rows, :] = jnp.dot(sel_s[buf, rows, :], h, preferred_element_type=f32).astype(bf16)

    def extra_blocks(tile, row0):
        def extra(b, _):
            one_hot(tile, b, slot_ref, row0, 2)
            pick(row0, 2)
            start_buf(tile, b, 2)
            wait_buf(2)
            return 0

        lax.fori_loop(1, jnp.maximum(_blocks_needed(cnt_sm, tile, ne), 1), extra, 0)

    t0 = 2 * j
    t1 = t0 + 1
    t2 = jnp.minimum(t0 + 2, n_tiles - 1)

    @pl.when(j == 0)
    def _():
        one_hot(0, 0, slot_ref, 0, 0)

    pick(0, 0)
    one_hot(t1, 0, slot_ref, t, 1)

    @pl.when(j > 0)
    def _():
        wait_buf(1)

    start_buf(t0, 0, 0)
    extra_blocks(t0, 0)

    pick(t, 1)
    one_hot(t2, 0, slot_nx, 0, 0)
    wait_buf(0)
    start_buf(t1, 0, 1)
    extra_blocks(t1, t)

    @pl.when(j == last)
    def _():
        wait_buf(1)

    @pl.when(j == last)
    def _():
        zeros_s[...] = jnp.zeros_like(zeros_s)

        def fill_copy(e, start, rows):
            return pltpu.make_async_copy(zeros_s.at[pl.ds(0, rows), :], xe_hbm.at[e, pl.ds(start, rows), :],
                                         sem.at[3])

        small = BF16_SUBLANES
        for e in range(ne):
            used = used_sm[e]
            n_big = (rows_alloc - used) // r
            n_small = (rows_alloc - used - n_big * r) // small

            def fill_big(k, _, e=e, used=used):
                fill_copy(e, pl.multiple_of(used + k * r, small), r).start()
                return 0

            def fill_small(k, _, e=e, used=used, n_big=n_big):
                fill_copy(e, pl.multiple_of(used + n_big * r + k * small, small), small).start()
                return 0

            def drain_big(k, _, e=e):
                fill_copy(e, 0, r).wait()
                return 0

            def drain_small(k, _, e=e):
                fill_copy(e, 0, small).wait()
                return 0

            lax.fori_loop(0, n_big, fill_big, 0)
            lax.fori_loop(0, n_small, fill_small, 0)
            lax.fori_loop(0, n_big, drain_big, 0)
            lax.fori_loop(0, n_small, drain_small, 0)


def _gather(base, cnt, used, slot_t, h2, rows_alloc):
    ne, n = slot_t.shape
    d = h2.shape[1]
    t = MOE_TILE
    tiles = n // t
    group = 4
    return pl.pallas_call(
        functools.partial(_gather_kernel, ne=ne, group=group),
        grid_spec=pltpu.PrefetchScalarGridSpec(
            num_scalar_prefetch=3,
            grid=(tiles // 2,),
            in_specs=[pl.BlockSpec((ne, 2 * t), lambda j, *_: (0, j)),
                      pl.BlockSpec((ne, t), lambda j, *_: (0, jnp.minimum(2 * j + 2, tiles - 1))),
                      pl.BlockSpec((2 * t, d), lambda j, *_: (j, 0))],
            out_specs=pl.BlockSpec(memory_space=pl.ANY),
            scratch_shapes=[pltpu.VMEM((3, ne * SLOT_BLOCK, t), bf16),
                            pltpu.VMEM((3, ne * SLOT_BLOCK, d), bf16),
                            pltpu.VMEM((SLOT_BLOCK, d), bf16),
                            pltpu.SemaphoreType.DMA((4,))]),
        out_shape=jax.ShapeDtypeStruct((ne, rows_alloc, d), bf16),
        compiler_params=_params("arbitrary"),
        name="gather",
    )(base, cnt, used, slot_t, slot_t, h2)


def _ffn_kernel(used_sm, x_ref, wg_ref, wu_ref, wd_ref, o_ref):
    e = pl.program_id(0)
    tm = x_ref.shape[1]
    r0 = pl.program_id(1) * tm
    used = used_sm[e]

    @pl.when(r0 < used)
    def _():
        x = x_ref[0]
        a = jnp.dot(x, wg_ref[0], preferred_element_type=f32)
        b = jnp.dot(x, wu_ref[0], preferred_element_type=f32)
        hid = (_silu(a) * b).astype(bf16)
        o_ref[0] = jnp.dot(hid, wd_ref[0], preferred_element_type=f32).astype(bf16)

    @pl.when(r0 >= used)
    def _():
        o_ref[...] = jnp.zeros_like(o_ref)


def _ffn(used, xe, wg, wu, wd):
    ne, rows, d = xe.shape
    f = wg.shape[2]
    tm = FFN_TILE
    return pl.pallas_call(
        _ffn_kernel,
        grid_spec=pltpu.PrefetchScalarGridSpec(
            num_scalar_prefetch=1,
            grid=(ne, rows // tm),
            in_specs=[pl.BlockSpec((1, tm, d), lambda e, i, *_: (e, i, 0)),
                      pl.BlockSpec((1, d, f), lambda e, i, *_: (e, 0, 0)),
                      pl.BlockSpec((1, d, f), lambda e, i, *_: (e, 0, 0)),
                      pl.BlockSpec((1, f, d), lambda e, i, *_: (e, 0, 0))],
            out_specs=pl.BlockSpec((1, tm, d), lambda e, i, *_: (e, i, 0))),
        out_shape=jax.ShapeDtypeStruct((ne, rows, d), bf16),
        compiler_params=_params("arbitrary", "arbitrary"),
        name="ffn",
    )(used, xe, wg, wu, wd)


def _combine_copy(ye_hbm, ybuf, sem, buf, e, start):
    r = SLOT_BLOCK
    return pltpu.make_async_copy(ye_hbm.at[e, pl.ds(start, r), :], ybuf.at[buf, pl.ds(e * SLOT_LANES, r), :],
                                 sem.at[buf])


def _combine_kernel(base_sm, cnt_sm, slot_ref, aff_ref, slot_nx, aff_nx, x1_ref, g2_ref, ye_hbm, o_ref,
                    sel_s, ybuf, sem, *, ne):
    j = pl.program_id(0)
    last = pl.num_programs(0) - 1
    r = SLOT_BLOCK
    t = MOE_TILE
    n_tiles = 2 * pl.num_programs(0)

    def start_block(tile, b, buf):
        for e in range(ne):
            start = pl.multiple_of(base_sm[tile * ne + e] + b * r, BF16_SUBLANES)
            _combine_copy(ye_hbm, ybuf, sem, buf, e, start).start()

    def wait_block(buf):
        for e in range(ne):
            _combine_copy(ye_hbm, ybuf, sem, buf, e, 0).wait()

    def one_hot(tile, b, sref, aref, row0, buf):
        lane = lax.broadcasted_iota(jnp.int32, (t, SLOT_LANES), 1)
        lane_iota = jnp.where(lane < r, lane, -(1 << 24)).astype(f32)
        rows = pl.ds(row0, t)
        for e in range(ne):
            start = (base_sm[tile * ne + e] + b * r).astype(f32)
            hit = sref[rows, e:e + 1] == lane_iota + start
            sel_s[buf, :, pl.ds(e * SLOT_LANES, SLOT_LANES)] = jnp.where(hit, aref[rows, e:e + 1], 0.0).astype(bf16)

    def finish(row0, buf):
        rows = pl.ds(row0, t)
        o_ref[rows, :] = x1_ref[rows, :] + g2_ref[0] * jnp.dot(sel_s[buf], ybuf[buf], preferred_element_type=f32)

    def extra_blocks(tile, row0):
        rows = pl.ds(row0, t)

        def extra(b, _):
            start_block(tile, b, 2)
            one_hot(tile, b, slot_ref, aff_ref, row0, 2)
            wait_block(2)
            o_ref[rows, :] += g2_ref[0] * jnp.dot(sel_s[2], ybuf[2], preferred_element_type=f32)
            return 0

        lax.fori_loop(1, jnp.maximum(_blocks_needed(cnt_sm, tile, ne), 1), extra, 0)

    t0 = 2 * j
    t1 = t0 + 1
    t2 = jnp.minimum(t0 + 2, n_tiles - 1)

    @pl.when(j == 0)
    def _():
        ybuf[...] = jnp.zeros(ybuf.shape, bf16)
        one_hot(0, 0, slot_ref, aff_ref, 0, 0)
        start_block(0, 0, 0)

    start_block(t1, 0, 1)
    wait_block(0)
    finish(0, 0)
    one_hot(t1, 0, slot_ref, aff_ref, t, 1)
    extra_blocks(t0, 0)

    @pl.when(j < last)
    def _():
        start_block(t2, 0, 0)

    wait_block(1)
    finish(t, 1)
    one_hot(t2, 0, slot_nx, aff_nx, 0, 0)
    extra_blocks(t1, t)


def _combine(base, cnt, slot, aff, x1, gate2, ye, seq):
    n, ne = slot.shape
    d = x1.shape[1]
    t = MOE_TILE
    tiles = n // t
    per_seq = seq // (2 * t)
    nxt = lambda j, *_: (jnp.minimum(2 * j + 2, tiles - 1), 0)
    return pl.pallas_call(
        functools.partial(_combine_kernel, ne=ne),
        grid_spec=pltpu.PrefetchScalarGridSpec(
            num_scalar_prefetch=2,
            grid=(tiles // 2,),
            in_specs=[pl.BlockSpec((2 * t, ne), lambda j, *_: (j, 0)),
                      pl.BlockSpec((2 * t, ne), lambda j, *_: (j, 0)),
                      pl.BlockSpec((t, ne), nxt),
                      pl.BlockSpec((t, ne), nxt),
                      pl.BlockSpec((2 * t, d), lambda j, *_: (j, 0)),
                      pl.BlockSpec((1, 1, d), lambda j, *_: (j // per_seq, 0, 0)),
                      pl.BlockSpec(memory_space=pl.ANY)],
            out_specs=pl.BlockSpec((2 * t, d), lambda j, *_: (j, 0)),
            scratch_shapes=[pltpu.VMEM((3, t, ne * SLOT_LANES), bf16),
                            pltpu.VMEM((3, ne * SLOT_LANES, d), bf16),
                            pltpu.SemaphoreType.DMA((3,))]),
        out_shape=jax.ShapeDtypeStruct((n, d), f32),
        compiler_params=_params("arbitrary"),
        name="combine",
    )(base, cnt, slot, aff, slot, aff, x1, gate2, ye)


def _rope_tables(s):
    t = jnp.arange(s)
    row = (t // GRID_W).astype(f32)
    col = (t % GRID_W).astype(f32)
    axis_dim = HEAD_DIM // 2
    inv = ROPE_THETA ** (-jnp.arange(0, axis_dim, 2, dtype=f32) / axis_dim)
    ang = jnp.concatenate([row[:, None] * inv, col[:, None] * inv], axis=-1)
    cos, sin = jnp.cos(ang), jnp.sin(ang)
    reps = LANES // HEAD_DIM
    cos_t = jnp.tile(jnp.concatenate([cos, cos], axis=-1), (1, reps))
    sin_t = jnp.tile(jnp.concatenate([-sin, sin], axis=-1), (1, reps))
    return cos_t, sin_t


def _head_perm(n_heads):
    half = jnp.concatenate([jnp.arange(0, HEAD_DIM, 2), jnp.arange(1, HEAD_DIM, 2)])
    return (jnp.arange(n_heads)[:, None] * HEAD_DIM + half[None, :]).reshape(-1)


def _block_mean(width):
    idx = jnp.arange(width) // HEAD_DIM
    return jnp.where(idx[:, None] == idx[None, :], 1.0 / HEAD_DIM, 0.0).astype(bf16)


def _prepare(norm_mix, w_in, q_norm, k_norm, conv_w, conv_b, conv_ln_g, conv_ln_b, w_out, norm_ffn,
             w_router, w_gate, w_up, w_down):
    d = w_in.shape[0]
    d_conv = conv_w.shape[-1]
    d_attn = w_out.shape[0] - d_conv
    d_kv = (w_in.shape[1] - d_attn - 2 * d_conv) // 2
    n_heads = d_attn // HEAD_DIM
    n_kv = d_kv // HEAD_DIM
    qp = _head_perm(n_heads)
    kp = _head_perm(n_kv)
    cols = jnp.concatenate([qp, d_attn + kp, jnp.arange(d_attn + d_kv, w_in.shape[1])])
    wr_hi = w_router.astype(bf16)
    wr_lo = (w_router - wr_hi.astype(f32)).astype(bf16)
    score_bound = HEAD_DIM * (HEAD_DIM ** -0.5 * LOG2E) * jnp.max(jnp.abs(q_norm)) * jnp.max(jnp.abs(k_norm))
    return dict(
        d_attn=d_attn, d_kv=d_kv, score_bound=score_bound,
        norm_mix=norm_mix.reshape(1, d),
        w_in=w_in[:, cols].astype(bf16),
        qg=jnp.tile(q_norm[_head_perm(1)], n_heads).reshape(1, d_attn),
        kg=jnp.tile(k_norm[_head_perm(1)], n_kv).reshape(1, d_kv),
        bdq=_block_mean(d_attn), bdk=_block_mean(d_kv),
        conv_w=conv_w.reshape(conv_w.shape[0], d_conv), conv_b=conv_b.reshape(1, d_conv),
        conv_ln_g=conv_ln_g.reshape(1, d_conv), conv_ln_b=conv_ln_b.reshape(1, d_conv),
        w_a=w_out[:d_attn].astype(bf16), w_c=w_out[d_attn:].astype(bf16),
        norm_ffn=norm_ffn.reshape(1, d),
        wr1=jnp.concatenate([wr_hi, wr_lo], axis=1), wr2=wr_hi,
        w_gate=w_gate.astype(bf16), w_up=w_up.astype(bf16), w_down=w_down.astype(bf16),
    )


def _encoder_layer(x, mod, p):
    b, s, d = x.shape
    shift1, scale1, gate1, shift2, scale2, gate2 = [m.reshape(b, 1, d) for m in jnp.split(mod, 6, axis=-1)]
    cos, sin = _rope_tables(s)
    q, k, v, u = _inproj(x, shift1, scale1, p["norm_mix"], p["w_in"], p["qg"], p["kg"], p["bdq"], p["bdk"],
                         cos, sin, p["d_attn"], p["d_kv"])
    attn = lax.cond(p["score_bound"] <= SCORE_BOUND_LOG2,
                    lambda: _attention(q, k, v, True), lambda: _attention(q, k, v, False))
    conv = _conv(u, p["conv_w"], p["conv_b"], p["conv_ln_g"], p["conv_ln_b"])
    x1, h2, aff = _outproj(attn, conv, x, gate1, scale2, shift2, p["norm_ffn"], p["w_a"], p["w_c"],
                           p["wr1"], p["wr2"])

    n = b * s
    ne = aff.shape[-1]
    cap = CAPACITY_FACTOR * n // ne
    aff = aff.reshape(n, ne)
    bits = lax.bitcast_convert_type(aff, jnp.int32).reshape(n * ne // LANES, LANES)
    thr_bits, ntie = _thresh(bits, cap, ne)
    thr = lax.bitcast_convert_type(thr_bits[0, :ne], f32).reshape(ne, 1)
    ntie = ntie[0, :ne].astype(f32).reshape(ne, 1)
    t = MOE_TILE
    tok = jnp.arange(t)
    ut = (tok[:, None] < tok[None, :]).astype(bf16)
    slot_t, cnt, base, used = _plan(aff.T, thr, ntie, ut)
    cnt = cnt.reshape(-1)
    base = base.reshape(-1)
    tiles = n // t
    rows_alloc = -(-(cap + BF16_SUBLANES * tiles + SLOT_BLOCK) // FFN_TILE) * FFN_TILE
    used = used.reshape(-1)
    xe = _gather(base, cnt, used, slot_t, h2.reshape(n, d), rows_alloc)
    ye = _ffn(used, xe, p["w_gate"], p["w_up"], p["w_down"])
    out = _combine(base, cnt, slot_t.T, aff, x1.reshape(n, d), gate2, ye, s)
    return out.reshape(b, s, d)


def kernel(x_prompt, x_sample, c_prompt, c_sample, ada_w, ada_b, norm_mix, w_in, q_norm, k_norm, conv_w, conv_b,
           conv_ln_g, conv_ln_b, w_out, norm_ffn, w_router, w_gate, w_up, w_down):
    y_prompt, y_sample = x_prompt, x_sample
    nb = x_prompt.shape[0]
    for l in range(ada_w.shape[0]):
        p = _prepare(norm_mix[l], w_in[l], q_norm[l], k_norm[l], conv_w[l], conv_b[l], conv_ln_g[l],
                     conv_ln_b[l], w_out[l], norm_ffn[l], w_router[l], w_gate[l], w_up[l], w_down[l])
        mod = _ada(jnp.concatenate([c_prompt, c_sample], axis=0), ada_w[l], ada_b[l])
        y_prompt = _encoder_layer(y_prompt, mod[:nb], p)
        y_sample = _encoder_layer(y_sample, mod[nb:], p)
    return (y_prompt, y_sample)
```

```python
import functools

import jax
import jax.numpy as jnp
from jax import lax
from jax.experimental import pallas as pl
from jax.experimental.pallas import tpu as pltpu

HEAD_DIM = 64
N_KV_HEADS = 2
GRID_W = 64
ROPE_THETA = 10000.0
CAPACITY_FACTOR = 2
EPS = 1e-6
LOG2E = 1.4426950408889634

LANES = 128
BF16_SUBLANES = 16
VMEM_LIMIT_BYTES = 48 * 1024 * 1024

TOKEN_TILE = 512
MOE_TILE = 512
SLOT_BLOCK = 128
SLOT_LANES = LANES
FFN_TILE = 512
Q_TILE = 256
K_TILE = 256
ATTN_STRIP = 256
SCORE_BOUND_LOG2 = 56.0
ATTN_BLOCKS_PER_TRIP = 8
CONV_TILE = 512
CONV_HALO = 16
CONV_ROWS = 64

f32 = jnp.float32
bf16 = jnp.bfloat16


def _params(*sem):
    return pltpu.CompilerParams(dimension_semantics=sem, vmem_limit_bytes=VMEM_LIMIT_BYTES)


def _silu(x):
    return x * jax.nn.sigmoid(x)


def _ada_kernel(c_ref, w_ref, b_ref, o_ref):
    c = c_ref[...]
    o_ref[...] = jnp.dot(_silu(c), w_ref[...], preferred_element_type=f32,
                         precision=lax.Precision.HIGHEST) + b_ref[...]


def _ada(c, w, b):
    r, d = c.shape
    n = w.shape[1]
    tn = n // 4
    return pl.pallas_call(
        _ada_kernel,
        grid=(n // tn,),
        in_specs=[pl.BlockSpec((r, d), lambda j: (0, 0)),
                  pl.BlockSpec((d, tn), lambda j: (0, j)),
                  pl.BlockSpec((1, tn), lambda j: (0, j))],
        out_specs=pl.BlockSpec((r, tn), lambda j: (0, j)),
        out_shape=jax.ShapeDtypeStruct((r, n), f32),
        compiler_params=_params("arbitrary"),
        name="ada",
    )(c, w, b.reshape(1, n))


def _norm_rope(t, bd_ref, gain_ref, cos, sin, scale):
    ms = jnp.dot((t * t).astype(bf16), bd_ref[...], preferred_element_type=f32)
    tn = t * lax.rsqrt(ms + EPS) * gain_ref[...]
    w = t.shape[1]
    reps = w // LANES
    if reps > 1:
        cos = jnp.concatenate([cos] * reps, axis=1)
        sin = jnp.concatenate([sin] * reps, axis=1)
    lane = lax.broadcasted_iota(jnp.int32, tn.shape, 1)
    first_half = (lane % HEAD_DIM) < (HEAD_DIM // 2)
    partner = jnp.where(first_half, pltpu.roll(tn, w - HEAD_DIM // 2, 1), pltpu.roll(tn, HEAD_DIM // 2, 1))
    return (tn * cos + partner * sin) * scale


def _inproj_kernel(x_ref, sh_ref, sc_ref, g_ref, w_ref, qg_ref, kg_ref, bdq_ref, bdk_ref, cos_ref, sin_ref,
                   q_ref, k_ref, v_ref, u_ref, *, d_attn, d_kv):
    x = x_ref[0]
    ms = jnp.mean(x * x, axis=-1, keepdims=True)
    h = x * lax.rsqrt(ms + EPS) * g_ref[...]
    h = h * (1.0 + sc_ref[0]) + sh_ref[0]
    z = jnp.dot(h.astype(bf16), w_ref[...], preferred_element_type=f32)
    cos = cos_ref[...]
    sin = sin_ref[...]
    q = _norm_rope(z[:, :d_attn], bdq_ref, qg_ref, cos, sin, HEAD_DIM ** -0.5 * LOG2E)
    k = _norm_rope(z[:, d_attn:d_attn + d_kv], bdk_ref, kg_ref, cos, sin, 1.0)
    v = z[:, d_attn + d_kv:d_attn + 2 * d_kv]
    q_ref[0] = q.T.astype(bf16)
    vt = v.T.astype(bf16)
    for g in range(d_kv // HEAD_DIM):
        k_ref[0, g] = k[:, g * HEAD_DIM:(g + 1) * HEAD_DIM].astype(bf16)
        for j in range(v_ref.shape[2]):
            v_ref[0, g, j] = vt[g * HEAD_DIM:(g + 1) * HEAD_DIM, j * K_TILE:(j + 1) * K_TILE]
    u_ref[0] = z[:, d_attn + 2 * d_kv:].astype(bf16)


def _inproj(x, shift, scale, gain, w_in, qg, kg, bdq, bdk, cos, sin, d_attn, d_kv):
    b, s, d = x.shape
    d_in = w_in.shape[1]
    d_u = d_in - d_attn - 2 * d_kv
    n_kv = d_kv // HEAD_DIM
    tm = TOKEN_TILE
    const = lambda shape: pl.BlockSpec(shape, lambda bi, i: (0,) * len(shape))
    per_b = pl.BlockSpec((1, 1, d), lambda bi, i: (bi, 0, 0))
    return pl.pallas_call(
        functools.partial(_inproj_kernel, d_attn=d_attn, d_kv=d_kv),
        grid=(b, s // tm),
        in_specs=[pl.BlockSpec((1, tm, d), lambda bi, i: (bi, i, 0)), per_b, per_b,
                  const((1, d)), const((d, d_in)), const((1, d_attn)), const((1, d_kv)),
                  const((d_attn, d_attn)), const((d_kv, d_kv)),
                  pl.BlockSpec((tm, LANES), lambda bi, i: (i, 0)),
                  pl.BlockSpec((tm, LANES), lambda bi, i: (i, 0))],
        out_specs=[pl.BlockSpec((1, d_attn, tm), lambda bi, i: (bi, 0, i)),
                   pl.BlockSpec((1, n_kv, tm, HEAD_DIM), lambda bi, i: (bi, 0, i, 0)),
                   pl.BlockSpec((1, n_kv, tm // K_TILE, HEAD_DIM, K_TILE), lambda bi, i: (bi, 0, i, 0, 0)),
                   pl.BlockSpec((1, tm, d_u), lambda bi, i: (bi, i, 0))],
        out_shape=[jax.ShapeDtypeStruct((b, d_attn, s), bf16),
                   jax.ShapeDtypeStruct((b, n_kv, s, HEAD_DIM), bf16),
                   jax.ShapeDtypeStruct((b, n_kv, s // K_TILE, HEAD_DIM, K_TILE), bf16),
                   jax.ShapeDtypeStruct((b, s, d_u), bf16)],
        compiler_params=_params("arbitrary", "arbitrary"),
        name="inproj",
    )(x, shift, scale, gain, w_in, qg, kg, bdq, bdk, cos, sin)


def _attn_kernel(q_ref, k_ref, v_ref, o_ref, s_scr, q_scr, m_scr, acc_scr, l_scr, *, n_rep, bounded):
    tq = q_ref.shape[2]
    n_blk, _, tk = v_ref.shape[2:]
    qt = q_ref[0]
    qs = jnp.concatenate([qt[h * HEAD_DIM:(h + 1) * HEAD_DIM, :] for h in range(n_rep)], axis=1)
    cols = qs.shape[1]
    ones = jnp.ones((BF16_SUBLANES, tk), bf16)

    strips = [pl.ds(c, ATTN_STRIP) for c in range(0, cols, ATTN_STRIP)]

    def scores(i, slot):
        kt = k_ref[0, 0, pl.ds(pl.multiple_of(jnp.minimum(i, n_blk - 1) * tk, tk), tk), :]
        for c in strips:
            s_scr[slot, :, c] = jnp.dot(kt, q_scr[:, c], preferred_element_type=f32)

    def attend(i, slot):
        vt = None if bounded else jnp.concatenate([v_ref[0, 0, i], ones], axis=0)
        for c in strips:
            if bounded:
                p = jnp.exp2(s_scr[slot, :, c])
                l_scr[:, c] += jnp.sum(p.reshape(tk // 8, 8, ATTN_STRIP), axis=0)
                acc_scr[pl.ds(0, HEAD_DIM), c] += jnp.dot(v_ref[0, 0, i], p.astype(bf16),
                                                          preferred_element_type=f32)
            else:
                m = m_scr[:, c]
                m_new = jnp.maximum(m, jnp.max(s_scr[slot, :, c], axis=0, keepdims=True))
                p = jnp.exp2(s_scr[slot, :, c] - m_new).astype(bf16)
                acc_scr[:, c] = jnp.exp2(m - m_new) * acc_scr[:, c] + jnp.dot(vt, p, preferred_element_type=f32)
                m_scr[:, c] = m_new

    per_trip = min(ATTN_BLOCKS_PER_TRIP, n_blk)

    def body(j, _):
        for step in range(per_trip):
            i = j * per_trip + step
            slot = step % 2
            scores(i + 1, 1 - slot)
            attend(i, slot)
        return 0

    q_scr[...] = qs
    m_scr[...] = jnp.full(m_scr.shape, -jnp.inf, f32)
    acc_scr[...] = jnp.zeros(acc_scr.shape, f32)
    l_scr[...] = jnp.zeros(l_scr.shape, f32)
    scores(0, 0)
    lax.fori_loop(0, n_blk // per_trip, body, 0)
    acc = acc_scr[...]
    denom = jnp.sum(l_scr[...], axis=0, keepdims=True) if bounded else acc[HEAD_DIM:HEAD_DIM + 1]
    o = acc[:HEAD_DIM] / denom
    o_ref[0] = jnp.concatenate([o[:, h * tq:(h + 1) * tq] for h in range(n_rep)], axis=0).astype(bf16)


def _attention(qt, k, vt, bounded):
    b, d_attn, s = qt.shape
    n_kv, n_blk, _, tk = vt.shape[1:]
    n_rep = d_attn // HEAD_DIM // n_kv
    tq = min(Q_TILE, s)
    gw = n_rep * HEAD_DIM
    return pl.pallas_call(
        functools.partial(_attn_kernel, n_rep=n_rep, bounded=bounded),
        grid=(b, n_kv, s // tq),
        in_specs=[pl.BlockSpec((1, gw, tq), lambda bi, g, i: (bi, g, i)),
                  pl.BlockSpec((1, 1, s, HEAD_DIM), lambda bi, g, i: (bi, g, 0, 0)),
                  pl.BlockSpec((1, 1, n_blk, HEAD_DIM, tk), lambda bi, g, i: (bi, g, 0, 0, 0))],
        out_specs=pl.BlockSpec((1, gw, tq), lambda bi, g, i: (bi, g, i)),
        out_shape=jax.ShapeDtypeStruct((b, d_attn, s), bf16),
        scratch_shapes=[pltpu.VMEM((2, tk, n_rep * tq), f32),
                        pltpu.VMEM((HEAD_DIM, n_rep * tq), bf16), pltpu.VMEM((1, n_rep * tq), f32),
                        pltpu.VMEM((HEAD_DIM + BF16_SUBLANES, n_rep * tq), f32),
                        pltpu.VMEM((8, n_rep * tq), f32)],
        compiler_params=_params("arbitrary", "arbitrary", "arbitrary"),
        name="attn_bounded" if bounded else "attn",
    )(qt, k, vt)


def _conv_kernel(u_ref, up_ref, un_ref, w_ref, cb_ref, lg_ref, lb_ref, o_ref, a_ref, y_ref, *, ts, dc, width):
    i = pl.program_id(1)
    last = pl.num_programs(1) - 1
    halo = CONV_HALO

    def glu(u):
        u = u.astype(f32)
        return u[:, :dc] * jax.nn.sigmoid(u[:, dc:])

    a_ref[pl.ds(halo, ts), :] = glu(u_ref[0])
    a_ref[pl.ds(0, halo), :] = jnp.where(i > 0, glu(up_ref[0]), 0.0)
    a_ref[pl.ds(halo + ts, halo), :] = jnp.where(i < last, glu(un_ref[0]), 0.0)

    off = halo - width // 2
    sub = 8
    rows = CONV_ROWS
    reach = -(-(off + width) // sub) * sub
    cb = cb_ref[...]
    lg = lg_ref[...]
    lb = lb_ref[...]
    for c in range(ts // rows):
        r0 = c * rows
        for g in range(dc // LANES):
            lanes = pl.ds(g * LANES, LANES)
            blk = a_ref[pl.ds(r0, rows + reach), lanes]
            acc = None
            for r in range(sub):
                part = None
                for u in range(r, off + width, sub):
                    if u < off:
                        continue
                    term = blk[u - r:u - r + rows + sub, :] * w_ref[pl.ds(u - off, 1), lanes]
                    part = term if part is None else part + term
                shifted = part[r:r + rows, :]
                acc = shifted if acc is None else acc + shifted
            y_ref[pl.ds(r0, rows), lanes] = acc
        y = y_ref[pl.ds(r0, rows), :] + cb
        mu = jnp.mean(y, axis=-1, keepdims=True)
        yc = y - mu
        var = jnp.mean(yc * yc, axis=-1, keepdims=True)
        yn = yc * lax.rsqrt(var + EPS) * lg + lb
        o_ref[0, pl.ds(r0, CONV_ROWS), :] = _silu(yn).astype(bf16)


def _conv(u, w, cb, lg, lb):
    b, s, du = u.shape
    dc = du // 2
    width = w.shape[0]
    ts = min(CONV_TILE, s)
    hb = ts // CONV_HALO
    n_h = s // CONV_HALO
    const = lambda shape: pl.BlockSpec(shape, lambda bi, i: (0,) * len(shape))
    return pl.pallas_call(
        functools.partial(_conv_kernel, ts=ts, dc=dc, width=width),
        grid=(b, s // ts),
        in_specs=[pl.BlockSpec((1, ts, du), lambda bi, i: (bi, i, 0)),
                  pl.BlockSpec((1, CONV_HALO, du), lambda bi, i: (bi, jnp.maximum(i * hb - 1, 0), 0)),
                  pl.BlockSpec((1, CONV_HALO, du), lambda bi, i: (bi, jnp.minimum((i + 1) * hb, n_h - 1), 0)),
                  const((width, dc)), const((1, dc)), const((1, dc)), const((1, dc))],
        out_specs=pl.BlockSpec((1, ts, dc), lambda bi, i: (bi, i, 0)),
        out_shape=jax.ShapeDtypeStruct((b, s, dc), bf16),
        scratch_shapes=[pltpu.VMEM((ts + 2 * CONV_HALO, dc), f32), pltpu.VMEM((ts, dc), f32)],
        compiler_params=_params("arbitrary", "arbitrary"),
        name="conv",
    )(u, u, u, w, cb, lg, lb)


def _outproj_kernel(a_ref, c_ref, x_ref, g1_ref, sc_ref, sh_ref, nf_ref, wa_ref, wc_ref, wr1_ref, wr2_ref,
                    x1_ref, h2_ref, aff_ref, *, ne):
    attn = a_ref[0].astype(f32).T.astype(bf16)
    mix = (jnp.dot(attn, wa_ref[...], preferred_element_type=f32)
           + jnp.dot(c_ref[0], wc_ref[...], preferred_element_type=f32))
    x1 = x_ref[0] + g1_ref[0] * mix
    x1_ref[0] = x1
    ms = jnp.mean(x1 * x1, axis=-1, keepdims=True)
    h = x1 * lax.rsqrt(ms + EPS) * nf_ref[...]
    h2 = h * (1.0 + sc_ref[0]) + sh_ref[0]
    hi = h2.astype(bf16)
    lo = (h2 - hi.astype(f32)).astype(bf16)
    h2_ref[0] = hi
    r1 = jnp.dot(hi, wr1_ref[...], preferred_element_type=f32)
    r2 = jnp.dot(lo, wr2_ref[...], preferred_element_type=f32)
    logits = r1[:, :ne] + r1[:, ne:] + r2
    m = jnp.max(logits, axis=-1, keepdims=True)
    e = jnp.exp(logits - m)
    aff_ref[0] = e / jnp.sum(e, axis=-1, keepdims=True)


def _outproj(attn, conv, x, gate1, scale2, shift2, norm_ffn, w_a, w_c, wr1, wr2):
    b, s, d = x.shape
    da = attn.shape[1]
    dc = conv.shape[2]
    ne = wr2.shape[1]
    tm = TOKEN_TILE
    const = lambda shape: pl.BlockSpec(shape, lambda bi, i: (0,) * len(shape))
    per_b = pl.BlockSpec((1, 1, d), lambda bi, i: (bi, 0, 0))
    return pl.pallas_call(
        functools.partial(_outproj_kernel, ne=ne),
        grid=(b, s // tm),
        in_specs=[pl.BlockSpec((1, da, tm), lambda bi, i: (bi, 0, i)),
                  pl.BlockSpec((1, tm, dc), lambda bi, i: (bi, i, 0)),
                  pl.BlockSpec((1, tm, d), lambda bi, i: (bi, i, 0)),
                  per_b, per_b, per_b, const((1, d)), const((da, d)), const((dc, d)),
                  const((d, 2 * ne)), const((d, ne))],
        out_specs=[pl.BlockSpec((1, tm, d), lambda bi, i: (bi, i, 0)),
                   pl.BlockSpec((1, tm, d), lambda bi, i: (bi, i, 0)),
                   pl.BlockSpec((1, tm, ne), lambda bi, i: (bi, i, 0))],
        out_shape=[jax.ShapeDtypeStruct((b, s, d), f32),
                   jax.ShapeDtypeStruct((b, s, d), bf16),
                   jax.ShapeDtypeStruct((b, s, ne), f32)],
        compiler_params=_params("arbitrary", "arbitrary"),
        name="outproj",
    )(attn, conv, x, gate1, scale2, shift2, norm_ffn, w_a, w_c, wr1, wr2)


def _thresh_kernel(bits_ref, thr_ref, ntie_ref, *, cap, ne, chunk):
    rows = bits_ref.shape[0]

    def count_ge(cand):
        cand_row = cand[0:1, :]

        def body(i, acc):
            blk = bits_ref[pl.ds(pl.multiple_of(i * chunk, chunk), chunk), :]
            hit = jnp.where(blk >= cand_row, 1, 0).astype(jnp.int32)
            return acc + jnp.sum(hit.reshape(chunk // 8, 8, LANES), axis=0)
        acc = lax.fori_loop(0, rows // chunk, body, jnp.zeros((8, LANES), jnp.int32))
        tot = jnp.sum(acc, axis=0, keepdims=True)
        tot = jnp.broadcast_to(tot, (8, LANES))
        shift = LANES // 2
        while shift >= ne:
            tot = tot + pltpu.roll(tot, shift, 1)
            shift //= 2
        return tot

    def bit_step(i, prefix):
        cand = prefix | jnp.left_shift(jnp.int32(1), 30 - i)
        return jnp.where(count_ge(cand) >= cap, cand, prefix)

    prefix = lax.fori_loop(0, 31, bit_step, jnp.zeros((8, LANES), jnp.int32))
    thr_ref[...] = prefix
    ntie_ref[...] = cap - count_ge(prefix + 1)


def _thresh(aff_bits, cap, ne):
    rows = aff_bits.shape[0]
    chunk = min(512, rows)
    return pl.pallas_call(
        functools.partial(_thresh_kernel, cap=cap, ne=ne, chunk=chunk),
        out_shape=[jax.ShapeDtypeStruct((8, LANES), jnp.int32)] * 2,
        compiler_params=pltpu.CompilerParams(vmem_limit_bytes=VMEM_LIMIT_BYTES),
        name="thresh",
    )(aff_bits)


def _plan_kernel(aff_ref, thr_ref, ntie_ref, ut_ref, slot_ref, cnt_ref, base_ref, used_ref, base_s, tie_s):
    @pl.when(pl.program_id(0) == 0)
    def _():
        base_s[...] = jnp.zeros_like(base_s)
        tie_s[...] = jnp.zeros_like(tie_s)

    a = aff_ref[...]
    thr = thr_ref[...]
    eq = a == thr
    eqf = jnp.where(eq, 1.0, 0.0)
    ut = ut_ref[...]
    tie_rank = tie_s[...] + jnp.dot(eqf.astype(bf16), ut, preferred_element_type=f32)
    sel = (a > thr) | (eq & (tie_rank < ntie_ref[...]))
    self_ = jnp.where(sel, 1.0, 0.0)
    rank = jnp.dot(self_.astype(bf16), ut, preferred_element_type=f32)
    cnt = jnp.sum(self_, axis=1, keepdims=True)
    base = base_s[...]
    slot_ref[...] = jnp.where(sel, base + rank, -1.0)
    cnt_ref[0] = cnt.astype(jnp.int32)
    base_ref[0] = base.astype(jnp.int32)
    new_base = base + jnp.ceil(cnt * (1.0 / BF16_SUBLANES)) * BF16_SUBLANES
    base_s[...] = new_base
    used_ref[...] = new_base.astype(jnp.int32)
    tie_s[...] = tie_s[...] + jnp.sum(eqf, axis=1, keepdims=True)


def _plan(aff_t, thr, ntie, ut):
    ne, n = aff_t.shape
    t = MOE_TILE
    tiles = n // t
    return pl.pallas_call(
        _plan_kernel,
        grid=(tiles,),
        in_specs=[pl.BlockSpec((ne, t), lambda i: (0, i)),
                  pl.BlockSpec((ne, 1), lambda i: (0, 0)),
                  pl.BlockSpec((ne, 1), lambda i: (0, 0)),
                  pl.BlockSpec((t, t), lambda i: (0, 0))],
        out_specs=[pl.BlockSpec((ne, t), lambda i: (0, i)),
                   pl.BlockSpec((1, ne, 1), lambda i: (i, 0, 0)),
                   pl.BlockSpec((1, ne, 1), lambda i: (i, 0, 0)),
                   pl.BlockSpec((ne, 1), lambda i: (0, 0))],
        out_shape=[jax.ShapeDtypeStruct((ne, n), f32),
                   jax.ShapeDtypeStruct((tiles, ne, 1), jnp.int32),
                   jax.ShapeDtypeStruct((tiles, ne, 1), jnp.int32),
                   jax.ShapeDtypeStruct((ne, 1), jnp.int32)],
        scratch_shapes=[pltpu.VMEM((ne, 1), f32), pltpu.VMEM((ne, 1), f32)],
        compiler_params=_params("arbitrary"),
        name="plan",
    )(aff_t, thr, ntie, ut)


def _blocks_needed(cnt_sm, tile, ne):
    mx = cnt_sm[tile * ne]
    for e in range(1, ne):
        mx = jnp.maximum(mx, cnt_sm[tile * ne + e])
    return (mx + SLOT_BLOCK - 1) // SLOT_BLOCK


def _gather_copy(stage, xe_hbm, sem, buf, e, start):
    r = SLOT_BLOCK
    return pltpu.make_async_copy(stage.at[buf, pl.ds(e * r, r), :], xe_hbm.at[e, pl.ds(start, r), :], sem.at[buf])


def _gather_kernel(base_sm, cnt_sm, used_sm, slot_ref, slot_nx, h_ref, xe_hbm, sel_s, stage, zeros_s, sem,
                   *, ne, group):
    j = pl.program_id(0)
    last = pl.num_programs(0) - 1
    n_tiles = 2 * pl.num_programs(0)
    r = SLOT_BLOCK
    t = MOE_TILE
    rows_alloc = xe_hbm.shape[1]

    def wait_buf(buf):
        for e in range(ne):
            _gather_copy(stage, xe_hbm, sem, buf, e, 0).wait()

    def start_buf(tile, b, buf):
        for e in range(ne):
            start = pl.multiple_of(base_sm[tile * ne + e] + b * r, BF16_SUBLANES)
            _gather_copy(stage, xe_hbm, sem, buf, e, start).start()

    def one_hot(tile, b, sref, col0, buf):
        row_iota = lax.broadcasted_iota(jnp.int32, (r, t), 0).astype(f32)
        for e in range(ne):
            start = (base_sm[tile * ne + e] + b * r).astype(f32)
            hit = sref[e:e + 1, pl.ds(col0, t)] == row_iota + start
            sel_s[buf, pl.ds(e * r, r), :] = jnp.where(hit, 1.0, 0.0).astype(bf16)

    def pick(row0, buf):
        h = h_ref[pl.ds(row0, t), :]
        for c in range(ne // group):
            rows = pl.ds(c * group * r, group * r)
            stage[buf, rows, :] = jnp.dot(sel_s[buf, rows, :], h, preferred_element_type=f32).astype(bf16)

    def extra_blocks(tile, row0):
        def extra(b, _):
            one_hot(tile, b, slot_ref, row0, 2)
            pick(row0, 2)
            start_buf(tile, b, 2)
            wait_buf(2)
            return 0

        lax.fori_loop(1, jnp.maximum(_blocks_needed(cnt_sm, tile, ne), 1), extra, 0)

    t0 = 2 * j
    t1 = t0 + 1
    t2 = jnp.minimum(t0 + 2, n_tiles - 1)

    @pl.when(j == 0)
    def _():
        one_hot(0, 0, slot_ref, 0, 0)

    pick(0, 0)
    one_hot(t1, 0, slot_ref, t, 1)

    @pl.when(j > 0)
    def _():
        wait_buf(1)

    start_buf(t0, 0, 0)
    extra_blocks(t0, 0)

    pick(t, 1)
    one_hot(t2, 0, slot_nx, 0, 0)
    wait_buf(0)
    start_buf(t1, 0, 1)
    extra_blocks(t1, t)

    @pl.when(j == last)
    def _():
        wait_buf(1)

    @pl.when(j == last)
    def _():
        zeros_s[...] = jnp.zeros_like(zeros_s)

        def fill_copy(e, start, rows):
            return pltpu.make_async_copy(zeros_s.at[pl.ds(0, rows), :], xe_hbm.at[e, pl.ds(start, rows), :],
                                         sem.at[3])

        small = BF16_SUBLANES
        for e in range(ne):
            used = used_sm[e]
            n_big = (rows_alloc - used) // r
            n_small = (rows_alloc - used - n_big * r) // small

            def fill_big(k, _, e=e, used=used):
                fill_copy(e, pl.multiple_of(used + k * r, small), r).start()
                return 0

            def fill_small(k, _, e=e, used=used, n_big=n_big):
                fill_copy(e, pl.multiple_of(used + n_big * r + k * small, small), small).start()
                return 0

            def drain_big(k, _, e=e):
                fill_copy(e, 0, r).wait()
                return 0

            def drain_small(k, _, e=e):
                fill_copy(e, 0, small).wait()
                return 0

            lax.fori_loop(0, n_big, fill_big, 0)
            lax.fori_loop(0, n_small, fill_small, 0)
            lax.fori_loop(0, n_big, drain_big, 0)
            lax.fori_loop(0, n_small, drain_small, 0)


def _gather(base, cnt, used, slot_t, h2, rows_alloc):
    ne, n = slot_t.shape
    d = h2.shape[1]
    t = MOE_TILE
    tiles = n // t
    group = 4
    return pl.pallas_call(
        functools.partial(_gather_kernel, ne=ne, group=group),
        grid_spec=pltpu.PrefetchScalarGridSpec(
            num_scalar_prefetch=3,
            grid=(tiles // 2,),
            in_specs=[pl.BlockSpec((ne, 2 * t), lambda j, *_: (0, j)),
                      pl.BlockSpec((ne, t), lambda j, *_: (0, jnp.minimum(2 * j + 2, tiles - 1))),
                      pl.BlockSpec((2 * t, d), lambda j, *_: (j, 0))],
            out_specs=pl.BlockSpec(memory_space=pl.ANY),
            scratch_shapes=[pltpu.VMEM((3, ne * SLOT_BLOCK, t), bf16),
                            pltpu.VMEM((3, ne * SLOT_BLOCK, d), bf16),
                            pltpu.VMEM((SLOT_BLOCK, d), bf16),
                            pltpu.SemaphoreType.DMA((4,))]),
        out_shape=jax.ShapeDtypeStruct((ne, rows_alloc, d), bf16),
        compiler_params=_params("arbitrary"),
        name="gather",
    )(base, cnt, used, slot_t, slot_t, h2)


def _ffn_kernel(used_sm, x_ref, wg_ref, wu_ref, wd_ref, o_ref, wg_s, wu_s, wd_s):
    e = pl.program_id(0)
    tm = x_ref.shape[1]
    r0 = pl.program_id(1) * tm
    used = used_sm[e]

    @pl.when(pl.program_id(1) == 0)
    def _():
        wg_s[...] = wg_ref[0].astype(bf16)
        wu_s[...] = wu_ref[0].astype(bf16)
        wd_s[...] = wd_ref[0].astype(bf16)

    @pl.when(r0 < used)
    def _():
        x = x_ref[0]
        a = jnp.dot(x, wg_s[...], preferred_element_type=f32)
        b = jnp.dot(x, wu_s[...], preferred_element_type=f32)
        hid = (_silu(a) * b).astype(bf16)
        o_ref[0] = jnp.dot(hid, wd_s[...], preferred_element_type=f32).astype(bf16)

    @pl.when(r0 >= used)
    def _():
        o_ref[...] = jnp.zeros_like(o_ref)


def _ffn(used, xe, wg, wu, wd):
    ne, rows, d = xe.shape
    f = wg.shape[2]
    tm = FFN_TILE
    return pl.pallas_call(
        _ffn_kernel,
        grid_spec=pltpu.PrefetchScalarGridSpec(
            num_scalar_prefetch=1,
            grid=(ne, rows // tm),
            in_specs=[pl.BlockSpec((1, tm, d), lambda e, i, *_: (e, i, 0)),
                      pl.BlockSpec((1, d, f), lambda e, i, *_: (e, 0, 0)),
                      pl.BlockSpec((1, d, f), lambda e, i, *_: (e, 0, 0)),
                      pl.BlockSpec((1, f, d), lambda e, i, *_: (e, 0, 0))],
            out_specs=pl.BlockSpec((1, tm, d), lambda e, i, *_: (e, i, 0)),
            scratch_shapes=[pltpu.VMEM((d, f), bf16), pltpu.VMEM((d, f), bf16), pltpu.VMEM((f, d), bf16)]),
        out_shape=jax.ShapeDtypeStruct((ne, rows, d), bf16),
        compiler_params=_params("arbitrary", "arbitrary"),
        name="ffn",
    )(used, xe, wg, wu, wd)


def _combine_copy(ye_hbm, ybuf, sem, buf, e, start):
    r = SLOT_BLOCK
    return pltpu.make_async_copy(ye_hbm.at[e, pl.ds(start, r), :], ybuf.at[buf, pl.ds(e * SLOT_LANES, r), :],
                                 sem.at[buf])


def _combine_kernel(base_sm, cnt_sm, slot_ref, aff_ref, slot_nx, aff_nx, x1_ref, g2_ref, ye_hbm, o_ref,
                    sel_s, ybuf, sem, *, ne):
    j = pl.program_id(0)
    last = pl.num_programs(0) - 1
    r = SLOT_BLOCK
    t = MOE_TILE
    n_tiles = 2 * pl.num_programs(0)

    def start_block(tile, b, buf):
        for e in range(ne):
            start = pl.multiple_of(base_sm[tile * ne + e] + b * r, BF16_SUBLANES)
            _combine_copy(ye_hbm, ybuf, sem, buf, e, start).start()

    def wait_block(buf):
        for e in range(ne):
            _combine_copy(ye_hbm, ybuf, sem, buf, e, 0).wait()

    def one_hot(tile, b, sref, aref, row0, buf):
        lane = lax.broadcasted_iota(jnp.int32, (t, SLOT_LANES), 1)
        lane_iota = jnp.where(lane < r, lane, -(1 << 24)).astype(f32)
        rows = pl.ds(row0, t)
        for e in range(ne):
            start = (base_sm[tile * ne + e] + b * r).astype(f32)
            hit = sref[rows, e:e + 1] == lane_iota + start
            sel_s[buf, :, pl.ds(e * SLOT_LANES, SLOT_LANES)] = jnp.where(hit, aref[rows, e:e + 1], 0.0).astype(bf16)

    def finish(row0, buf):
        rows = pl.ds(row0, t)
        o_ref[rows, :] = x1_ref[rows, :] + g2_ref[0] * jnp.dot(sel_s[buf], ybuf[buf], preferred_element_type=f32)

    def extra_blocks(tile, row0):
        rows = pl.ds(row0, t)

        def extra(b, _):
            start_block(tile, b, 2)
            one_hot(tile, b, slot_ref, aff_ref, row0, 2)
            wait_block(2)
            o_ref[rows, :] += g2_ref[0] * jnp.dot(sel_s[2], ybuf[2], preferred_element_type=f32)
            return 0

        lax.fori_loop(1, jnp.maximum(_blocks_needed(cnt_sm, tile, ne), 1), extra, 0)

    t0 = 2 * j
    t1 = t0 + 1
    t2 = jnp.minimum(t0 + 2, n_tiles - 1)

    @pl.when(j == 0)
    def _():
        ybuf[...] = jnp.zeros(ybuf.shape, bf16)
        one_hot(0, 0, slot_ref, aff_ref, 0, 0)
        start_block(0, 0, 0)

    start_block(t1, 0, 1)
    wait_block(0)
    finish(0, 0)
    one_hot(t1, 0, slot_ref, aff_ref, t, 1)
    extra_blocks(t0, 0)

    @pl.when(j < last)
    def _():
        start_block(t2, 0, 0)

    wait_block(1)
    finish(t, 1)
    one_hot(t2, 0, slot_nx, aff_nx, 0, 0)
    extra_blocks(t1, t)


def _combine(base, cnt, slot, aff, x1, gate2, ye, seq):
    n, ne = slot.shape
    d = x1.shape[1]
    t = MOE_TILE
    tiles = n // t
    per_seq = seq // (2 * t)
    nxt = lambda j, *_: (jnp.minimum(2 * j + 2, tiles - 1), 0)
    return pl.pallas_call(
        functools.partial(_combine_kernel, ne=ne),
        grid_spec=pltpu.PrefetchScalarGridSpec(
            num_scalar_prefetch=2,
            grid=(tiles // 2,),
            in_specs=[pl.BlockSpec((2 * t, ne), lambda j, *_: (j, 0)),
                      pl.BlockSpec((2 * t, ne), lambda j, *_: (j, 0)),
                      pl.BlockSpec((t, ne), nxt),
                      pl.BlockSpec((t, ne), nxt),
                      pl.BlockSpec((2 * t, d), lambda j, *_: (j, 0)),
                      pl.BlockSpec((1, 1, d), lambda j, *_: (j // per_seq, 0, 0)),
                      pl.BlockSpec(memory_space=pl.ANY)],
            out_specs=pl.BlockSpec((2 * t, d), lambda j, *_: (j, 0)),
            scratch_shapes=[pltpu.VMEM((3, t, ne * SLOT_LANES), bf16),
                            pltpu.VMEM((3, ne * SLOT_LANES, d), bf16),
                            pltpu.SemaphoreType.DMA((3,))]),
        out_shape=jax.ShapeDtypeStruct((n, d), f32),
        compiler_params=_params("arbitrary"),
        name="combine",
    )(base, cnt, slot, aff, slot, aff, x1, gate2, ye)


def _rope_tables(s):
    t = jnp.arange(s)
    row = (t // GRID_W).astype(f32)
    col = (t % GRID_W).astype(f32)
    axis_dim = HEAD_DIM // 2
    inv = ROPE_THETA ** (-jnp.arange(0, axis_dim, 2, dtype=f32) / axis_dim)
    ang = jnp.concatenate([row[:, None] * inv, col[:, None] * inv], axis=-1)
    cos, sin = jnp.cos(ang), jnp.sin(ang)
    reps = LANES // HEAD_DIM
    cos_t = jnp.tile(jnp.concatenate([cos, cos], axis=-1), (1, reps))
    sin_t = jnp.tile(jnp.concatenate([-sin, sin], axis=-1), (1, reps))
    return cos_t, sin_t


def _head_perm(n_heads):
    half = jnp.concatenate([jnp.arange(0, HEAD_DIM, 2), jnp.arange(1, HEAD_DIM, 2)])
    return (jnp.arange(n_heads)[:, None] * HEAD_DIM + half[None, :]).reshape(-1)


def _block_mean(width):
    idx = jnp.arange(width) // HEAD_DIM
    return jnp.where(idx[:, None] == idx[None, :], 1.0 / HEAD_DIM, 0.0).astype(bf16)


def _prepare(norm_mix, w_in, q_norm, k_norm, conv_w, conv_b, conv_ln_g, conv_ln_b, w_out, norm_ffn,
             w_router, w_gate, w_up, w_down):
    d = w_in.shape[0]
    d_conv = conv_w.shape[-1]
    d_attn = w_out.shape[0] - d_conv
    d_kv = (w_in.shape[1] - d_attn - 2 * d_conv) // 2
    n_heads = d_attn // HEAD_DIM
    n_kv = d_kv // HEAD_DIM
    qp = _head_perm(n_heads)
    kp = _head_perm(n_kv)
    cols = jnp.concatenate([qp, d_attn + kp, jnp.arange(d_attn + d_kv, w_in.shape[1])])
    wr_hi = w_router.astype(bf16)
    wr_lo = (w_router - wr_hi.astype(f32)).astype(bf16)
    score_bound = HEAD_DIM * (HEAD_DIM ** -0.5 * LOG2E) * jnp.max(jnp.abs(q_norm)) * jnp.max(jnp.abs(k_norm))
    return dict(
        d_attn=d_attn, d_kv=d_kv, score_bound=score_bound,
        norm_mix=norm_mix.reshape(1, d),
        w_in=w_in[:, cols].astype(bf16),
        qg=jnp.tile(q_norm[_head_perm(1)], n_heads).reshape(1, d_attn),
        kg=jnp.tile(k_norm[_head_perm(1)], n_kv).reshape(1, d_kv),
        bdq=_block_mean(d_attn), bdk=_block_mean(d_kv),
        conv_w=conv_w.reshape(conv_w.shape[0], d_conv), conv_b=conv_b.reshape(1, d_conv),
        conv_ln_g=conv_ln_g.reshape(1, d_conv), conv_ln_b=conv_ln_b.reshape(1, d_conv),
        w_a=w_out[:d_attn].astype(bf16), w_c=w_out[d_attn:].astype(bf16),
        norm_ffn=norm_ffn.reshape(1, d),
        wr1=jnp.concatenate([wr_hi, wr_lo], axis=1), wr2=wr_hi,
        w_gate=w_gate, w_up=w_up, w_down=w_down,
    )


def _encoder_layer(x, mod, p):
    b, s, d = x.shape
    shift1, scale1, gate1, shift2, scale2, gate2 = [m.reshape(b, 1, d) for m in jnp.split(mod, 6, axis=-1)]
    cos, sin = _rope_tables(s)
    q, k, v, u = _inproj(x, shift1, scale1, p["norm_mix"], p["w_in"], p["qg"], p["kg"], p["bdq"], p["bdk"],
                         cos, sin, p["d_attn"], p["d_kv"])
    attn = lax.cond(p["score_bound"] <= SCORE_BOUND_LOG2,
                    lambda: _attention(q, k, v, True), lambda: _attention(q, k, v, False))
    conv = _conv(u, p["conv_w"], p["conv_b"], p["conv_ln_g"], p["conv_ln_b"])
    x1, h2, aff = _outproj(attn, conv, x, gate1, scale2, shift2, p["norm_ffn"], p["w_a"], p["w_c"],
                           p["wr1"], p["wr2"])

    n = b * s
    ne = aff.shape[-1]
    cap = CAPACITY_FACTOR * n // ne
    aff = aff.reshape(n, ne)
    bits = lax.bitcast_convert_type(aff, jnp.int32).reshape(n * ne // LANES, LANES)
    thr_bits, ntie = _thresh(bits, cap, ne)
    thr = lax.bitcast_convert_type(thr_bits[0, :ne], f32).reshape(ne, 1)
    ntie = ntie[0, :ne].astype(f32).reshape(ne, 1)
    t = MOE_TILE
    tok = jnp.arange(t)
    ut = (tok[:, None] < tok[None, :]).astype(bf16)
    slot_t, cnt, base, used = _plan(aff.T, thr, ntie, ut)
    cnt = cnt.reshape(-1)
    base = base.reshape(-1)
    tiles = n // t
    rows_alloc = -(-(cap + BF16_SUBLANES * tiles + SLOT_BLOCK) // FFN_TILE) * FFN_TILE
    used = used.reshape(-1)
    xe = _gather(base, cnt, used, slot_t, h2.reshape(n, d), rows_alloc)
    ye = _ffn(used, xe, p["w_gate"], p["w_up"], p["w_down"])
    out = _combine(base, cnt, slot_t.T, aff, x1.reshape(n, d), gate2, ye, s)
    return out.reshape(b, s, d)


def kernel(x_prompt, x_sample, c_prompt, c_sample, ada_w, ada_b, norm_mix, w_in, q_norm, k_norm, conv_w, conv_b,
           conv_ln_g, conv_ln_b, w_out, norm_ffn, w_router, w_gate, w_up, w_down):
    y_prompt, y_sample = x_prompt, x_sample
    nb = x_prompt.shape[0]
    for l in range(ada_w.shape[0]):
        p = _prepare(norm_mix[l], w_in[l], q_norm[l], k_norm[l], conv_w[l], conv_b[l], conv_ln_g[l],
                     conv_ln_b[l], w_out[l], norm_ffn[l], w_router[l], w_gate[l], w_up[l], w_down[l])
        mod = _ada(jnp.concatenate([c_prompt, c_sample], axis=0), ada_w[l], ada_b[l])
        y_prompt = _encoder_layer(y_prompt, mod[:nb], p)
        y_sample = _encoder_layer(y_sample, mod[nb:], p)
    return (y_prompt, y_sample)
```

```python
import functools

import jax
import jax.numpy as jnp
from jax import lax
from jax.experimental import pallas as pl
from jax.experimental.pallas import tpu as pltpu

HEAD_DIM = 64
N_KV_HEADS = 2
GRID_W = 64
ROPE_THETA = 10000.0
CAPACITY_FACTOR = 2
EPS = 1e-6
LOG2E = 1.4426950408889634

LANES = 128
BF16_SUBLANES = 16
VMEM_LIMIT_BYTES = 48 * 1024 * 1024

TOKEN_TILE = 512
MOE_TILE = 512
SLOT_BLOCK = 128
SLOT_LANES = LANES
FFN_TILE = 512
Q_TILE = 256
K_TILE = 256
ATTN_STRIP = 256
SCORE_BOUND_LOG2 = 56.0
ATTN_BLOCKS_PER_TRIP = 16
CONV_TILE = 512
CONV_HALO = 16
CONV_ROWS = 64

f32 = jnp.float32
bf16 = jnp.bfloat16


def _params(*sem):
    return pltpu.CompilerParams(dimension_semantics=sem, vmem_limit_bytes=VMEM_LIMIT_BYTES)


def _silu(x):
    return x * jax.nn.sigmoid(x)


def _ada_kernel(c_ref, w_ref, b_ref, o_ref):
    c = c_ref[...]
    o_ref[...] = jnp.dot(_silu(c), w_ref[...], preferred_element_type=f32,
                         precision=lax.Precision.HIGHEST) + b_ref[...]


def _ada(c, w, b):
    r, d = c.shape
    n = w.shape[1]
    tn = n // 4
    return pl.pallas_call(
        _ada_kernel,
        grid=(n // tn,),
        in_specs=[pl.BlockSpec((r, d), lambda j: (0, 0)),
                  pl.BlockSpec((d, tn), lambda j: (0, j)),
                  pl.BlockSpec((1, tn), lambda j: (0, j))],
        out_specs=pl.BlockSpec((r, tn), lambda j: (0, j)),
        out_shape=jax.ShapeDtypeStruct((r, n), f32),
        compiler_params=_params("arbitrary"),
        name="ada",
    )(c, w, b.reshape(1, n))


def _norm_rope(t, bd_ref, gain_ref, cos, sin, scale):
    ms = jnp.dot((t * t).astype(bf16), bd_ref[...], preferred_element_type=f32)
    tn = t * lax.rsqrt(ms + EPS) * gain_ref[...]
    w = t.shape[1]
    reps = w // LANES
    if reps > 1:
        cos = jnp.concatenate([cos] * reps, axis=1)
        sin = jnp.concatenate([sin] * reps, axis=1)
    lane = lax.broadcasted_iota(jnp.int32, tn.shape, 1)
    first_half = (lane % HEAD_DIM) < (HEAD_DIM // 2)
    partner = jnp.where(first_half, pltpu.roll(tn, w - HEAD_DIM // 2, 1), pltpu.roll(tn, HEAD_DIM // 2, 1))
    return (tn * cos + partner * sin) * scale


def _inproj_kernel(x_ref, sh_ref, sc_ref, g_ref, w_ref, qg_ref, kg_ref, bdq_ref, bdk_ref, cos_ref, sin_ref,
                   q_ref, k_ref, v_ref, u_ref, *, d_attn, d_kv):
    x = x_ref[0]
    ms = jnp.mean(x * x, axis=-1, keepdims=True)
    h = x * lax.rsqrt(ms + EPS) * g_ref[...]
    h = h * (1.0 + sc_ref[0]) + sh_ref[0]
    z = jnp.dot(h.astype(bf16), w_ref[...], preferred_element_type=f32)
    cos = cos_ref[...]
    sin = sin_ref[...]
    q = _norm_rope(z[:, :d_attn], bdq_ref, qg_ref, cos, sin, HEAD_DIM ** -0.5 * LOG2E)
    k = _norm_rope(z[:, d_attn:d_attn + d_kv], bdk_ref, kg_ref, cos, sin, 1.0)
    v = z[:, d_attn + d_kv:d_attn + 2 * d_kv]
    q_ref[0] = q.T.astype(bf16)
    vt = v.T.astype(bf16)
    for g in range(d_kv // HEAD_DIM):
        k_ref[0, g] = k[:, g * HEAD_DIM:(g + 1) * HEAD_DIM].astype(bf16)
        for j in range(v_ref.shape[2]):
            v_ref[0, g, j] = vt[g * HEAD_DIM:(g + 1) * HEAD_DIM, j * K_TILE:(j + 1) * K_TILE]
    u_ref[0] = z[:, d_attn + 2 * d_kv:].astype(bf16)


def _inproj(x, shift, scale, gain, w_in, qg, kg, bdq, bdk, cos, sin, d_attn, d_kv):
    b, s, d = x.shape
    d_in = w_in.shape[1]
    d_u = d_in - d_attn - 2 * d_kv
    n_kv = d_kv // HEAD_DIM
    tm = TOKEN_TILE
    const = lambda shape: pl.BlockSpec(shape, lambda bi, i: (0,) * len(shape))
    per_b = pl.BlockSpec((1, 1, d), lambda bi, i: (bi, 0, 0))
    return pl.pallas_call(
        functools.partial(_inproj_kernel, d_attn=d_attn, d_kv=d_kv),
        grid=(b, s // tm),
        in_specs=[pl.BlockSpec((1, tm, d), lambda bi, i: (bi, i, 0)), per_b, per_b,
                  const((1, d)), const((d, d_in)), const((1, d_attn)), const((1, d_kv)),
                  const((d_attn, d_attn)), const((d_kv, d_kv)),
                  pl.BlockSpec((tm, LANES), lambda bi, i: (i, 0)),
                  pl.BlockSpec((tm, LANES), lambda bi, i: (i, 0))],
        out_specs=[pl.BlockSpec((1, d_attn, tm), lambda bi, i: (bi, 0, i)),
                   pl.BlockSpec((1, n_kv, tm, HEAD_DIM), lambda bi, i: (bi, 0, i, 0)),
                   pl.BlockSpec((1, n_kv, tm // K_TILE, HEAD_DIM, K_TILE), lambda bi, i: (bi, 0, i, 0, 0)),
                   pl.BlockSpec((1, tm, d_u), lambda bi, i: (bi, i, 0))],
        out_shape=[jax.ShapeDtypeStruct((b, d_attn, s), bf16),
                   jax.ShapeDtypeStruct((b, n_kv, s, HEAD_DIM), bf16),
                   jax.ShapeDtypeStruct((b, n_kv, s // K_TILE, HEAD_DIM, K_TILE), bf16),
                   jax.ShapeDtypeStruct((b, s, d_u), bf16)],
        compiler_params=_params("arbitrary", "arbitrary"),
        name="inproj",
    )(x, shift, scale, gain, w_in, qg, kg, bdq, bdk, cos, sin)


def _attn_kernel(q_ref, k_ref, v_ref, o_ref, s_scr, q_scr, m_scr, acc_scr, *, n_rep, bounded):
    tq = q_ref.shape[2]
    n_blk, _, tk = v_ref.shape[2:]
    qt = q_ref[0]
    qs = jnp.concatenate([qt[h * HEAD_DIM:(h + 1) * HEAD_DIM, :] for h in range(n_rep)], axis=1)
    cols = qs.shape[1]
    ones = jnp.ones((BF16_SUBLANES, tk), bf16)

    strips = [pl.ds(c, ATTN_STRIP) for c in range(0, cols, ATTN_STRIP)]

    def scores(i, slot):
        kt = k_ref[0, 0, pl.ds(pl.multiple_of(jnp.minimum(i, n_blk - 1) * tk, tk), tk), :]
        for c in strips:
            s_scr[slot, :, c] = jnp.dot(kt, q_scr[:, c], preferred_element_type=f32)

    def attend(i, slot):
        vt = jnp.concatenate([v_ref[0, 0, i], ones], axis=0)
        for c in strips:
            if bounded:
                p = jnp.exp2(s_scr[slot, :, c]).astype(bf16)
                acc_scr[:, c] += jnp.dot(vt, p, preferred_element_type=f32)
            else:
                m = m_scr[:, c]
                m_new = jnp.maximum(m, jnp.max(s_scr[slot, :, c], axis=0, keepdims=True))
                p = jnp.exp2(s_scr[slot, :, c] - m_new).astype(bf16)
                acc_scr[:, c] = jnp.exp2(m - m_new) * acc_scr[:, c] + jnp.dot(vt, p, preferred_element_type=f32)
                m_scr[:, c] = m_new

    per_trip = min(ATTN_BLOCKS_PER_TRIP, n_blk)

    def body(j, _):
        for step in range(per_trip):
            i = j * per_trip + step
            slot = step % 2
            scores(i + 1, 1 - slot)
            attend(i, slot)
        return 0

    q_scr[...] = qs
    m_scr[...] = jnp.full(m_scr.shape, -jnp.inf, f32)
    acc_scr[...] = jnp.zeros(acc_scr.shape, f32)
    scores(0, 0)
    lax.fori_loop(0, n_blk // per_trip, body, 0)
    acc = acc_scr[...]
    o = acc[:HEAD_DIM] / acc[HEAD_DIM:HEAD_DIM + 1]
    o_ref[0] = jnp.concatenate([o[:, h * tq:(h + 1) * tq] for h in range(n_rep)], axis=0).astype(bf16)


def _attention(qt, k, vt, bounded):
    b, d_attn, s = qt.shape
    n_kv, n_blk, _, tk = vt.shape[1:]
    n_rep = d_attn // HEAD_DIM // n_kv
    tq = min(Q_TILE, s)
    gw = n_rep * HEAD_DIM
    return pl.pallas_call(
        functools.partial(_attn_kernel, n_rep=n_rep, bounded=bounded),
        grid=(b, n_kv, s // tq),
        in_specs=[pl.BlockSpec((1, gw, tq), lambda bi, g, i: (bi, g, i)),
                  pl.BlockSpec((1, 1, s, HEAD_DIM), lambda bi, g, i: (bi, g, 0, 0)),
                  pl.BlockSpec((1, 1, n_blk, HEAD_DIM, tk), lambda bi, g, i: (bi, g, 0, 0, 0))],
        out_specs=pl.BlockSpec((1, gw, tq), lambda bi, g, i: (bi, g, i)),
        out_shape=jax.ShapeDtypeStruct((b, d_attn, s), bf16),
        scratch_shapes=[pltpu.VMEM((2, tk, n_rep * tq), f32),
                        pltpu.VMEM((HEAD_DIM, n_rep * tq), bf16), pltpu.VMEM((1, n_rep * tq), f32),
                        pltpu.VMEM((HEAD_DIM + BF16_SUBLANES, n_rep * tq), f32)],
        compiler_params=_params("arbitrary", "arbitrary", "arbitrary"),
        name="attn_bounded" if bounded else "attn",
    )(qt, k, vt)


def _conv_kernel(u_ref, up_ref, un_ref, w_ref, cb_ref, lg_ref, lb_ref, o_ref, a_ref, y_ref, *, ts, dc, width):
    i = pl.program_id(1)
    last = pl.num_programs(1) - 1
    halo = CONV_HALO

    def glu(u):
        u = u.astype(f32)
        return u[:, :dc] * jax.nn.sigmoid(u[:, dc:])

    a_ref[pl.ds(halo, ts), :] = glu(u_ref[0])
    a_ref[pl.ds(0, halo), :] = jnp.where(i > 0, glu(up_ref[0]), 0.0)
    a_ref[pl.ds(halo + ts, halo), :] = jnp.where(i < last, glu(un_ref[0]), 0.0)

    off = halo - width // 2
    sub = 8
    rows = CONV_ROWS
    reach = -(-(off + width) // sub) * sub
    cb = cb_ref[...]
    lg = lg_ref[...]
    lb = lb_ref[...]
    for c in range(ts // rows):
        r0 = c * rows
        for g in range(dc // LANES):
            lanes = pl.ds(g * LANES, LANES)
            blk = a_ref[pl.ds(r0, rows + reach), lanes]
            acc = None
            for r in range(sub):
                part = None
                for u in range(r, off + width, sub):
                    if u < off:
                        continue
                    term = blk[u - r:u - r + rows + sub, :] * w_ref[pl.ds(u - off, 1), lanes]
                    part = term if part is None else part + term
                shifted = part[r:r + rows, :]
                acc = shifted if acc is None else acc + shifted
            y_ref[pl.ds(r0, rows), lanes] = acc
        y = y_ref[pl.ds(r0, rows), :] + cb
        mu = jnp.mean(y, axis=-1, keepdims=True)
        yc = y - mu
        var = jnp.mean(yc * yc, axis=-1, keepdims=True)
        yn = yc * lax.rsqrt(var + EPS) * lg + lb
        o_ref[0, pl.ds(r0, CONV_ROWS), :] = _silu(yn).astype(bf16)


def _conv(u, w, cb, lg, lb):
    b, s, du = u.shape
    dc = du // 2
    width = w.shape[0]
    ts = min(CONV_TILE, s)
    hb = ts // CONV_HALO
    n_h = s // CONV_HALO
    const = lambda shape: pl.BlockSpec(shape, lambda bi, i: (0,) * len(shape))
    return pl.pallas_call(
        functools.partial(_conv_kernel, ts=ts, dc=dc, width=width),
        grid=(b, s // ts),
        in_specs=[pl.BlockSpec((1, ts, du), lambda bi, i: (bi, i, 0)),
                  pl.BlockSpec((1, CONV_HALO, du), lambda bi, i: (bi, jnp.maximum(i * hb - 1, 0), 0)),
                  pl.BlockSpec((1, CONV_HALO, du), lambda bi, i: (bi, jnp.minimum((i + 1) * hb, n_h - 1), 0)),
                  const((width, dc)), const((1, dc)), const((1, dc)), const((1, dc))],
        out_specs=pl.BlockSpec((1, ts, dc), lambda bi, i: (bi, i, 0)),
        out_shape=jax.ShapeDtypeStruct((b, s, dc), bf16),
        scratch_shapes=[pltpu.VMEM((ts + 2 * CONV_HALO, dc), f32), pltpu.VMEM((ts, dc), f32)],
        compiler_params=_params("arbitrary", "arbitrary"),
        name="conv",
    )(u, u, u, w, cb, lg, lb)


def _outproj_kernel(a_ref, c_ref, x_ref, g1_ref, sc_ref, sh_ref, nf_ref, wa_ref, wc_ref, wr1_ref, wr2_ref,
                    x1_ref, h2_ref, aff_ref, *, ne):
    attn = a_ref[0].astype(f32).T.astype(bf16)
    mix = (jnp.dot(attn, wa_ref[...], preferred_element_type=f32)
           + jnp.dot(c_ref[0], wc_ref[...], preferred_element_type=f32))
    x1 = x_ref[0] + g1_ref[0] * mix
    x1_ref[0] = x1
    ms = jnp.mean(x1 * x1, axis=-1, keepdims=True)
    h = x1 * lax.rsqrt(ms + EPS) * nf_ref[...]
    h2 = h * (1.0 + sc_ref[0]) + sh_ref[0]
    hi = h2.astype(bf16)
    lo = (h2 - hi.astype(f32)).astype(bf16)
    h2_ref[0] = hi
    r1 = jnp.dot(hi, wr1_ref[...], preferred_element_type=f32)
    r2 = jnp.dot(lo, wr2_ref[...], preferred_element_type=f32)
    logits = r1[:, :ne] + r1[:, ne:] + r2
    m = jnp.max(logits, axis=-1, keepdims=True)
    e = jnp.exp(logits - m)
    aff_ref[0] = e / jnp.sum(e, axis=-1, keepdims=True)


def _outproj(attn, conv, x, gate1, scale2, shift2, norm_ffn, w_a, w_c, wr1, wr2):
    b, s, d = x.shape
    da = attn.shape[1]
    dc = conv.shape[2]
    ne = wr2.shape[1]
    tm = TOKEN_TILE
    const = lambda shape: pl.BlockSpec(shape, lambda bi, i: (0,) * len(shape))
    per_b = pl.BlockSpec((1, 1, d), lambda bi, i: (bi, 0, 0))
    return pl.pallas_call(
        functools.partial(_outproj_kernel, ne=ne),
        grid=(b, s // tm),
        in_specs=[pl.BlockSpec((1, da, tm), lambda bi, i: (bi, 0, i)),
                  pl.BlockSpec((1, tm, dc), lambda bi, i: (bi, i, 0)),
                  pl.BlockSpec((1, tm, d), lambda bi, i: (bi, i, 0)),
                  per_b, per_b, per_b, const((1, d)), const((da, d)), const((dc, d)),
                  const((d, 2 * ne)), const((d, ne))],
        out_specs=[pl.BlockSpec((1, tm, d), lambda bi, i: (bi, i, 0)),
                   pl.BlockSpec((1, tm, d), lambda bi, i: (bi, i, 0)),
                   pl.BlockSpec((1, tm, ne), lambda bi, i: (bi, i, 0))],
        out_shape=[jax.ShapeDtypeStruct((b, s, d), f32),
                   jax.ShapeDtypeStruct((b, s, d), bf16),
                   jax.ShapeDtypeStruct((b, s, ne), f32)],
        compiler_params=_params("arbitrary", "arbitrary"),
        name="outproj",
    )(attn, conv, x, gate1, scale2, shift2, norm_ffn, w_a, w_c, wr1, wr2)


def _thresh_kernel(bits_ref, thr_ref, ntie_ref, *, cap, ne, chunk):
    rows = bits_ref.shape[0]

    def count_ge(cand):
        cand_row = cand[0:1, :]

        def body(i, acc):
            blk = bits_ref[pl.ds(pl.multiple_of(i * chunk, chunk), chunk), :]
            hit = jnp.where(blk >= cand_row, 1, 0).astype(jnp.int32)
            return acc + jnp.sum(hit.reshape(chunk // 8, 8, LANES), axis=0)
        acc = lax.fori_loop(0, rows // chunk, body, jnp.zeros((8, LANES), jnp.int32))
        tot = jnp.sum(acc, axis=0, keepdims=True)
        tot = jnp.broadcast_to(tot, (8, LANES))
        shift = LANES // 2
        while shift >= ne:
            tot = tot + pltpu.roll(tot, shift, 1)
            shift //= 2
        return tot

    def bit_step(i, prefix):
        cand = prefix | jnp.left_shift(jnp.int32(1), 30 - i)
        return jnp.where(count_ge(cand) >= cap, cand, prefix)

    prefix = lax.fori_loop(0, 31, bit_step, jnp.zeros((8, LANES), jnp.int32))
    thr_ref[...] = prefix
    ntie_ref[...] = cap - count_ge(prefix + 1)


def _thresh(aff_bits, cap, ne):
    rows = aff_bits.shape[0]
    chunk = min(512, rows)
    return pl.pallas_call(
        functools.partial(_thresh_kernel, cap=cap, ne=ne, chunk=chunk),
        out_shape=[jax.ShapeDtypeStruct((8, LANES), jnp.int32)] * 2,
        compiler_params=pltpu.CompilerParams(vmem_limit_bytes=VMEM_LIMIT_BYTES),
        name="thresh",
    )(aff_bits)


def _plan_kernel(aff_ref, thr_ref, ntie_ref, ut_ref, slot_ref, cnt_ref, base_ref, used_ref, base_s, tie_s):
    @pl.when(pl.program_id(0) == 0)
    def _():
        base_s[...] = jnp.zeros_like(base_s)
        tie_s[...] = jnp.zeros_like(tie_s)

    a = aff_ref[...]
    thr = thr_ref[...]
    eq = a == thr
    eqf = jnp.where(eq, 1.0, 0.0)
    ut = ut_ref[...]
    tie_rank = tie_s[...] + jnp.dot(eqf.astype(bf16), ut, preferred_element_type=f32)
    sel = (a > thr) | (eq & (tie_rank < ntie_ref[...]))
    self_ = jnp.where(sel, 1.0, 0.0)
    rank = jnp.dot(self_.astype(bf16), ut, preferred_element_type=f32)
    cnt = jnp.sum(self_, axis=1, keepdims=True)
    base = base_s[...]
    slot_ref[...] = jnp.where(sel, base + rank, -1.0)
    cnt_ref[0] = cnt.astype(jnp.int32)
    base_ref[0] = base.astype(jnp.int32)
    new_base = base + jnp.ceil(cnt * (1.0 / BF16_SUBLANES)) * BF16_SUBLANES
    base_s[...] = new_base
    used_ref[...] = new_base.astype(jnp.int32)
    tie_s[...] = tie_s[...] + jnp.sum(eqf, axis=1, keepdims=True)


def _plan(aff_t, thr, ntie, ut):
    ne, n = aff_t.shape
    t = MOE_TILE
    tiles = n // t
    return pl.pallas_call(
        _plan_kernel,
        grid=(tiles,),
        in_specs=[pl.BlockSpec((ne, t), lambda i: (0, i)),
                  pl.BlockSpec((ne, 1), lambda i: (0, 0)),
                  pl.BlockSpec((ne, 1), lambda i: (0, 0)),
                  pl.BlockSpec((t, t), lambda i: (0, 0))],
        out_specs=[pl.BlockSpec((ne, t), lambda i: (0, i)),
                   pl.BlockSpec((1, ne, 1), lambda i: (i, 0, 0)),
                   pl.BlockSpec((1, ne, 1), lambda i: (i, 0, 0)),
                   pl.BlockSpec((ne, 1), lambda i: (0, 0))],
        out_shape=[jax.ShapeDtypeStruct((ne, n), f32),
                   jax.ShapeDtypeStruct((tiles, ne, 1), jnp.int32),
                   jax.ShapeDtypeStruct((tiles, ne, 1), jnp.int32),
                   jax.ShapeDtypeStruct((ne, 1), jnp.int32)],
        scratch_shapes=[pltpu.VMEM((ne, 1), f32), pltpu.VMEM((ne, 1), f32)],
        compiler_params=_params("arbitrary"),
        name="plan",
    )(aff_t, thr, ntie, ut)


def _blocks_needed(cnt_sm, tile, ne):
    mx = cnt_sm[tile * ne]
    for e in range(1, ne):
        mx = jnp.maximum(mx, cnt_sm[tile * ne + e])
    return (mx + SLOT_BLOCK - 1) // SLOT_BLOCK


def _gather_copy(stage, xe_hbm, sem, buf, e, start):
    r = SLOT_BLOCK
    return pltpu.make_async_copy(stage.at[buf, pl.ds(e * r, r), :], xe_hbm.at[e, pl.ds(start, r), :], sem.at[buf])


def _gather_kernel(base_sm, cnt_sm, used_sm, slot_ref, slot_nx, h_ref, xe_hbm, sel_s, stage, zeros_s, sem,
                   *, ne, group):
    j = pl.program_id(0)
    last = pl.num_programs(0) - 1
    n_tiles = 2 * pl.num_programs(0)
    r = SLOT_BLOCK
    t = MOE_TILE
    rows_alloc = xe_hbm.shape[1]

    def wait_buf(buf):
        for e in range(ne):
            _gather_copy(stage, xe_hbm, sem, buf, e, 0).wait()

    def start_buf(tile, b, buf):
        for e in range(ne):
            start = pl.multiple_of(base_sm[tile * ne + e] + b * r, BF16_SUBLANES)
            _gather_copy(stage, xe_hbm, sem, buf, e, start).start()

    def one_hot(tile, b, sref, col0, buf):
        row_iota = lax.broadcasted_iota(jnp.int32, (r, t), 0).astype(f32)
        for e in range(ne):
            start = (base_sm[tile * ne + e] + b * r).astype(f32)
            hit = sref[e:e + 1, pl.ds(col0, t)] == row_iota + start
            sel_s[buf, pl.ds(e * r, r), :] = jnp.where(hit, 1.0, 0.0).astype(bf16)

    def pick(row0, buf):
        h = h_ref[pl.ds(row0, t), :]
        for c in range(ne // group):
            rows = pl.ds(c * group * r, group * r)
            stage[buf, rows, :] = jnp.dot(sel_s[buf, rows, :], h, preferred_element_type=f32).astype(bf16)

    def extra_blocks(tile, row0):
        def extra(b, _):
            one_hot(tile, b, slot_ref, row0, 2)
            pick(row0, 2)
            start_buf(tile, b, 2)
            wait_buf(2)
            return 0

        lax.fori_loop(1, jnp.maximum(_blocks_needed(cnt_sm, tile, ne), 1), extra, 0)

    t0 = 2 * j
    t1 = t0 + 1
    t2 = jnp.minimum(t0 + 2, n_tiles - 1)

    @pl.when(j == 0)
    def _():
        one_hot(0, 0, slot_ref, 0, 0)

    pick(0, 0)
    one_hot(t1, 0, slot_ref, t, 1)

    @pl.when(j > 0)
    def _():
        wait_buf(1)

    start_buf(t0, 0, 0)
    extra_blocks(t0, 0)

    pick(t, 1)
    one_hot(t2, 0, slot_nx, 0, 0)
    wait_buf(0)
    start_buf(t1, 0, 1)
    extra_blocks(t1, t)

    @pl.when(j == last)
    def _():
        wait_buf(1)

    @pl.when(j == last)
    def _():
        zeros_s[...] = jnp.zeros_like(zeros_s)

        def fill_copy(e, start, rows):
            return pltpu.make_async_copy(zeros_s.at[pl.ds(0, rows), :], xe_hbm.at[e, pl.ds(start, rows), :],
                                         sem.at[3])

        small = BF16_SUBLANES
        for e in range(ne):
            used = used_sm[e]
            n_big = (rows_alloc - used) // r
            n_small = (rows_alloc - used - n_big * r) // small

            def fill_big(k, _, e=e, used=used):
                fill_copy(e, pl.multiple_of(used + k * r, small), r).start()
                return 0

            def fill_small(k, _, e=e, used=used, n_big=n_big):
                fill_copy(e, pl.multiple_of(used + n_big * r + k * small, small), small).start()
                return 0

            def drain_big(k, _, e=e):
                fill_copy(e, 0, r).wait()
                return 0

            def drain_small(k, _, e=e):
                fill_copy(e, 0, small).wait()
                return 0

            lax.fori_loop(0, n_big, fill_big, 0)
            lax.fori_loop(0, n_small, fill_small, 0)
            lax.fori_loop(0, n_big, drain_big, 0)
            lax.fori_loop(0, n_small, drain_small, 0)


def _gather(base, cnt, used, slot_t, h2, rows_alloc):
    ne, n = slot_t.shape
    d = h2.shape[1]
    t = MOE_TILE
    tiles = n // t
    group = 4
    return pl.pallas_call(
        functools.partial(_gather_kernel, ne=ne, group=group),
        grid_spec=pltpu.PrefetchScalarGridSpec(
            num_scalar_prefetch=3,
            grid=(tiles // 2,),
            in_specs=[pl.BlockSpec((ne, 2 * t), lambda j, *_: (0, j)),
                      pl.BlockSpec((ne, t), lambda j, *_: (0, jnp.minimum(2 * j + 2, tiles - 1))),
                      pl.BlockSpec((2 * t, d), lambda j, *_: (j, 0))],
            out_specs=pl.BlockSpec(memory_space=pl.ANY),
            scratch_shapes=[pltpu.VMEM((3, ne * SLOT_BLOCK, t), bf16),
                            pltpu.VMEM((3, ne * SLOT_BLOCK, d), bf16),
                            pltpu.VMEM((SLOT_BLOCK, d), bf16),
                            pltpu.SemaphoreType.DMA((4,))]),
        out_shape=jax.ShapeDtypeStruct((ne, rows_alloc, d), bf16),
        compiler_params=_params("arbitrary"),
        name="gather",
    )(base, cnt, used, slot_t, slot_t, h2)


def _ffn_kernel(used_sm, x_ref, wg_ref, wu_ref, wd_ref, o_ref, wg_s, wu_s, wd_s):
    e = pl.program_id(0)
    tm = x_ref.shape[1]
    r0 = pl.program_id(1) * tm
    used = used_sm[e]

    @pl.when(pl.program_id(1) == 0)
    def _():
        wg_s[...] = wg_ref[0].astype(bf16)
        wu_s[...] = wu_ref[0].astype(bf16)
        wd_s[...] = wd_ref[0].astype(bf16)

    @pl.when(r0 < used)
    def _():
        x = x_ref[0]
        a = jnp.dot(x, wg_s[...], preferred_element_type=f32)
        b = jnp.dot(x, wu_s[...], preferred_element_type=f32)
        hid = (_silu(a) * b).astype(bf16)
        o_ref[0] = jnp.dot(hid, wd_s[...], preferred_element_type=f32).astype(bf16)

    @pl.when(r0 >= used)
    def _():
        o_ref[...] = jnp.zeros_like(o_ref)


def _ffn(used, xe, wg, wu, wd):
    ne, rows, d = xe.shape
    f = wg.shape[2]
    tm = FFN_TILE
    return pl.pallas_call(
        _ffn_kernel,
        grid_spec=pltpu.PrefetchScalarGridSpec(
            num_scalar_prefetch=1,
            grid=(ne, rows // tm),
            in_specs=[pl.BlockSpec((1, tm, d), lambda e, i, *_: (e, i, 0)),
                      pl.BlockSpec((1, d, f), lambda e, i, *_: (e, 0, 0)),
                      pl.BlockSpec((1, d, f), lambda e, i, *_: (e, 0, 0)),
                      pl.BlockSpec((1, f, d), lambda e, i, *_: (e, 0, 0))],
            out_specs=pl.BlockSpec((1, tm, d), lambda e, i, *_: (e, i, 0)),
            scratch_shapes=[pltpu.VMEM((d, f), bf16), pltpu.VMEM((d, f), bf16), pltpu.VMEM((f, d), bf16)]),
        out_shape=jax.ShapeDtypeStruct((ne, rows, d), bf16),
        compiler_params=_params("arbitrary", "arbitrary"),
        name="ffn",
    )(used, xe, wg, wu, wd)


def _combine_copy(ye_hbm, ybuf, sem, buf, e, start):
    r = SLOT_BLOCK
    return pltpu.make_async_copy(ye_hbm.at[e, pl.ds(start, r), :], ybuf.at[buf, pl.ds(e * SLOT_LANES, r), :],
                                 sem.at[buf])


def _combine_kernel(base_sm, cnt_sm, slot_ref, aff_ref, slot_nx, aff_nx, x1_ref, g2_ref, ye_hbm, o_ref,
                    sel_s, ybuf, sem, *, ne):
    j = pl.program_id(0)
    last = pl.num_programs(0) - 1
    r = SLOT_BLOCK
    t = MOE_TILE
    n_tiles = 2 * pl.num_programs(0)

    def start_block(tile, b, buf):
        for e in range(ne):
            start = pl.multiple_of(base_sm[tile * ne + e] + b * r, BF16_SUBLANES)
            _combine_copy(ye_hbm, ybuf, sem, buf, e, start).start()

    def wait_block(buf):
        for e in range(ne):
            _combine_copy(ye_hbm, ybuf, sem, buf, e, 0).wait()

    def one_hot(tile, b, sref, aref, row0, buf):
        lane = lax.broadcasted_iota(jnp.int32, (t, SLOT_LANES), 1)
        lane_iota = jnp.where(lane < r, lane, -(1 << 24)).astype(f32)
        rows = pl.ds(row0, t)
        for e in range(ne):
            start = (base_sm[tile * ne + e] + b * r).astype(f32)
            hit = sref[rows, e:e + 1] == lane_iota + start
            sel_s[buf, :, pl.ds(e * SLOT_LANES, SLOT_LANES)] = jnp.where(hit, aref[rows, e:e + 1], 0.0).astype(bf16)

    def finish(row0, buf):
        rows = pl.ds(row0, t)
        o_ref[rows, :] = x1_ref[rows, :] + g2_ref[0] * jnp.dot(sel_s[buf], ybuf[buf], preferred_element_type=f32)

    def extra_blocks(tile, row0):
        rows = pl.ds(row0, t)

        def extra(b, _):
            start_block(tile, b, 2)
            one_hot(tile, b, slot_ref, aff_ref, row0, 2)
            wait_block(2)
            o_ref[rows, :] += g2_ref[0] * jnp.dot(sel_s[2], ybuf[2], preferred_element_type=f32)
            return 0

        lax.fori_loop(1, jnp.maximum(_blocks_needed(cnt_sm, tile, ne), 1), extra, 0)

    t0 = 2 * j
    t1 = t0 + 1
    t2 = jnp.minimum(t0 + 2, n_tiles - 1)

    @pl.when(j == 0)
    def _():
        ybuf[...] = jnp.zeros(ybuf.shape, bf16)
        one_hot(0, 0, slot_ref, aff_ref, 0, 0)
        start_block(0, 0, 0)

    start_block(t1, 0, 1)
    wait_block(0)
    finish(0, 0)
    one_hot(t1, 0, slot_ref, aff_ref, t, 1)
    extra_blocks(t0, 0)

    @pl.when(j < last)
    def _():
        start_block(t2, 0, 0)

    wait_block(1)
    finish(t, 1)
    one_hot(t2, 0, slot_nx, aff_nx, 0, 0)
    extra_blocks(t1, t)


def _combine(base, cnt, slot, aff, x1, gate2, ye, seq):
    n, ne = slot.shape
    d = x1.shape[1]
    t = MOE_TILE
    tiles = n // t
    per_seq = seq // (2 * t)
    nxt = lambda j, *_: (jnp.minimum(2 * j + 2, tiles - 1), 0)
    return pl.pallas_call(
        functools.partial(_combine_kernel, ne=ne),
        grid_spec=pltpu.PrefetchScalarGridSpec(
            num_scalar_prefetch=2,
            grid=(tiles // 2,),
            in_specs=[pl.BlockSpec((2 * t, ne), lambda j, *_: (j, 0)),
                      pl.BlockSpec((2 * t, ne), lambda j, *_: (j, 0)),
                      pl.BlockSpec((t, ne), nxt),
                      pl.BlockSpec((t, ne), nxt),
                      pl.BlockSpec((2 * t, d), lambda j, *_: (j, 0)),
                      pl.BlockSpec((1, 1, d), lambda j, *_: (j // per_seq, 0, 0)),
                      pl.BlockSpec(memory_space=pl.ANY)],
            out_specs=pl.BlockSpec((2 * t, d), lambda j, *_: (j, 0)),
            scratch_shapes=[pltpu.VMEM((3, t, ne * SLOT_LANES), bf16),
                            pltpu.VMEM((3, ne * SLOT_LANES, d), bf16),
                            pltpu.SemaphoreType.DMA((3,))]),
        out_shape=jax.ShapeDtypeStruct((n, d), f32),
        compiler_params=_params("arbitrary"),
        name="combine",
    )(base, cnt, slot, aff, slot, aff, x1, gate2, ye)


def _rope_tables(s):
    t = jnp.arange(s)
    row = (t // GRID_W).astype(f32)
    col = (t % GRID_W).astype(f32)
    axis_dim = HEAD_DIM // 2
    inv = ROPE_THETA ** (-jnp.arange(0, axis_dim, 2, dtype=f32) / axis_dim)
    ang = jnp.concatenate([row[:, None] * inv, col[:, None] * inv], axis=-1)
    cos, sin = jnp.cos(ang), jnp.sin(ang)
    reps = LANES // HEAD_DIM
    cos_t = jnp.tile(jnp.concatenate([cos, cos], axis=-1), (1, reps))
    sin_t = jnp.tile(jnp.concatenate([-sin, sin], axis=-1), (1, reps))
    return cos_t, sin_t


def _head_perm(n_heads):
    half = jnp.concatenate([jnp.arange(0, HEAD_DIM, 2), jnp.arange(1, HEAD_DIM, 2)])
    return (jnp.arange(n_heads)[:, None] * HEAD_DIM + half[None, :]).reshape(-1)


def _block_mean(width):
    idx = jnp.arange(width) // HEAD_DIM
    return jnp.where(idx[:, None] == idx[None, :], 1.0 / HEAD_DIM, 0.0).astype(bf16)


def _prepare(norm_mix, w_in, q_norm, k_norm, conv_w, conv_b, conv_ln_g, conv_ln_b, w_out, norm_ffn,
             w_router, w_gate, w_up, w_down):
    d = w_in.shape[0]
    d_conv = conv_w.shape[-1]
    d_attn = w_out.shape[0] - d_conv
    d_kv = (w_in.shape[1] - d_attn - 2 * d_conv) // 2
    n_heads = d_attn // HEAD_DIM
    n_kv = d_kv // HEAD_DIM
    qp = _head_perm(n_heads)
    kp = _head_perm(n_kv)
    cols = jnp.concatenate([qp, d_attn + kp, jnp.arange(d_attn + d_kv, w_in.shape[1])])
    wr_hi = w_router.astype(bf16)
    wr_lo = (w_router - wr_hi.astype(f32)).astype(bf16)
    score_bound = HEAD_DIM * (HEAD_DIM ** -0.5 * LOG2E) * jnp.max(jnp.abs(q_norm)) * jnp.max(jnp.abs(k_norm))
    return dict(
        d_attn=d_attn, d_kv=d_kv, score_bound=score_bound,
        norm_mix=norm_mix.reshape(1, d),
        w_in=w_in[:, cols].astype(bf16),
        qg=jnp.tile(q_norm[_head_perm(1)], n_heads).reshape(1, d_attn),
        kg=jnp.tile(k_norm[_head_perm(1)], n_kv).reshape(1, d_kv),
        bdq=_block_mean(d_attn), bdk=_block_mean(d_kv),
        conv_w=conv_w.reshape(conv_w.shape[0], d_conv), conv_b=conv_b.reshape(1, d_conv),
        conv_ln_g=conv_ln_g.reshape(1, d_conv), conv_ln_b=conv_ln_b.reshape(1, d_conv),
        w_a=w_out[:d_attn].astype(bf16), w_c=w_out[d_attn:].astype(bf16),
        norm_ffn=norm_ffn.reshape(1, d),
        wr1=jnp.concatenate([wr_hi, wr_lo], axis=1), wr2=wr_hi,
        w_gate=w_gate, w_up=w_up, w_down=w_down,
    )


def _encoder_layer(x, mod, p):
    b, s, d = x.shape
    shift1, scale1, gate1, shift2, scale2, gate2 = [m.reshape(b, 1, d) for m in jnp.split(mod, 6, axis=-1)]
    cos, sin = _rope_tables(s)
    q, k, v, u = _inproj(x, shift1, scale1, p["norm_mix"], p["w_in"], p["qg"], p["kg"], p["bdq"], p["bdk"],
                         cos, sin, p["d_attn"], p["d_kv"])
    attn = lax.cond(p["score_bound"] <= SCORE_BOUND_LOG2,
                    lambda: _attention(q, k, v, True), lambda: _attention(q, k, v, False))
    conv = _conv(u, p["conv_w"], p["conv_b"], p["conv_ln_g"], p["conv_ln_b"])
    x1, h2, aff = _outproj(attn, conv, x, gate1, scale2, shift2, p["norm_ffn"], p["w_a"], p["w_c"],
                           p["wr1"], p["wr2"])

    n = b * s
    ne = aff.shape[-1]
    cap = CAPACITY_FACTOR * n // ne
    aff = aff.reshape(n, ne)
    bits = lax.bitcast_convert_type(aff, jnp.int32).reshape(n * ne // LANES, LANES)
    thr_bits, ntie = _thresh(bits, cap, ne)
    thr = lax.bitcast_convert_type(thr_bits[0, :ne], f32).reshape(ne, 1)
    ntie = ntie[0, :ne].astype(f32).reshape(ne, 1)
    t = MOE_TILE
    tok = jnp.arange(t)
    ut = (tok[:, None] < tok[None, :]).astype(bf16)
    slot_t, cnt, base, used = _plan(aff.T, thr, ntie, ut)
    cnt = cnt.reshape(-1)
    base = base.reshape(-1)
    tiles = n // t
    rows_alloc = -(-(cap + BF16_SUBLANES * tiles + SLOT_BLOCK) // FFN_TILE) * FFN_TILE
    used = used.reshape(-1)
    xe = _gather(base, cnt, used, slot_t, h2.reshape(n, d), rows_alloc)
    ye = _ffn(used, xe, p["w_gate"], p["w_up"], p["w_down"])
    out = _combine(base, cnt, slot_t.T, aff, x1.reshape(n, d), gate2, ye, s)
    return out.reshape(b, s, d)


def kernel(x_prompt, x_sample, c_prompt, c_sample, ada_w, ada_b, norm_mix, w_in, q_norm, k_norm, conv_w, conv_b,
           conv_ln_g, conv_ln_b, w_out, norm_ffn, w_router, w_gate, w_up, w_down):
    y_prompt, y_sample = x_prompt, x_sample
    nb = x_prompt.shape[0]
    for l in range(ada_w.shape[0]):
        p = _prepare(norm_mix[l], w_in[l], q_norm[l], k_norm[l], conv_w[l], conv_b[l], conv_ln_g[l],
                     conv_ln_b[l], w_out[l], norm_ffn[l], w_router[l], w_gate[l], w_up[l], w_down[l])
        mod = _ada(jnp.concatenate([c_prompt, c_sample], axis=0), ada_w[l], ada_b[l])
        y_prompt = _encoder_layer(y_prompt, mod[:nb], p)
        y_sample = _encoder_layer(y_sample, mod[nb:], p)
    return (y_prompt, y_sample)
```

```python
import functools

import jax
import jax.numpy as jnp
from jax import lax
from jax.experimental import pallas as pl
from jax.experimental.pallas import tpu as pltpu

HEAD_DIM = 64
N_KV_HEADS = 2
GRID_W = 64
ROPE_THETA = 10000.0
CAPACITY_FACTOR = 2
EPS = 1e-6
LOG2E = 1.4426950408889634

LANES = 128
BF16_SUBLANES = 16
VMEM_LIMIT_BYTES = 48 * 1024 * 1024

TOKEN_TILE = 512
MOE_TILE = 512
SLOT_BLOCK = 128
GATHER_ROWS = 96
SLOT_LANES = LANES
FFN_TILE = 512
Q_TILE = 256
K_TILE = 256
ATTN_STRIP = 256
SCORE_BOUND_LOG2 = 56.0
ATTN_BLOCKS_PER_TRIP = 16
CONV_TILE = 512
CONV_HALO = 16
CONV_ROWS = 64

f32 = jnp.float32
bf16 = jnp.bfloat16


def _params(*sem):
    return pltpu.CompilerParams(dimension_semantics=sem, vmem_limit_bytes=VMEM_LIMIT_BYTES)


def _silu(x):
    return x * jax.nn.sigmoid(x)


def _ada_kernel(c_ref, w_ref, b_ref, o_ref):
    c = c_ref[...]
    o_ref[...] = jnp.dot(_silu(c), w_ref[...], preferred_element_type=f32,
                         precision=lax.Precision.HIGHEST) + b_ref[...]


def _ada(c, w, b):
    r, d = c.shape
    n = w.shape[1]
    tn = n // 4
    return pl.pallas_call(
        _ada_kernel,
        grid=(n // tn,),
        in_specs=[pl.BlockSpec((r, d), lambda j: (0, 0)),
                  pl.BlockSpec((d, tn), lambda j: (0, j)),
                  pl.BlockSpec((1, tn), lambda j: (0, j))],
        out_specs=pl.BlockSpec((r, tn), lambda j: (0, j)),
        out_shape=jax.ShapeDtypeStruct((r, n), f32),
        compiler_params=_params("arbitrary"),
        name="ada",
    )(c, w, b.reshape(1, n))


def _norm_rope(t, bd_ref, gain_ref, cos, sin, scale):
    ms = jnp.dot((t * t).astype(bf16), bd_ref[...], preferred_element_type=f32)
    tn = t * lax.rsqrt(ms + EPS) * gain_ref[...]
    w = t.shape[1]
    reps = w // LANES
    if reps > 1:
        cos = jnp.concatenate([cos] * reps, axis=1)
        sin = jnp.concatenate([sin] * reps, axis=1)
    lane = lax.broadcasted_iota(jnp.int32, tn.shape, 1)
    first_half = (lane % HEAD_DIM) < (HEAD_DIM // 2)
    partner = jnp.where(first_half, pltpu.roll(tn, w - HEAD_DIM // 2, 1), pltpu.roll(tn, HEAD_DIM // 2, 1))
    return (tn * cos + partner * sin) * scale


def _inproj_kernel(x_ref, sh_ref, sc_ref, g_ref, w_ref, qg_ref, kg_ref, bdq_ref, bdk_ref, cos_ref, sin_ref,
                   q_ref, k_ref, v_ref, u_ref, *, d_attn, d_kv):
    x = x_ref[0]
    ms = jnp.mean(x * x, axis=-1, keepdims=True)
    h = x * lax.rsqrt(ms + EPS) * g_ref[...]
    h = h * (1.0 + sc_ref[0]) + sh_ref[0]
    z = jnp.dot(h.astype(bf16), w_ref[...], preferred_element_type=f32)
    cos = cos_ref[...]
    sin = sin_ref[...]
    q = _norm_rope(z[:, :d_attn], bdq_ref, qg_ref, cos, sin, HEAD_DIM ** -0.5 * LOG2E)
    k = _norm_rope(z[:, d_attn:d_attn + d_kv], bdk_ref, kg_ref, cos, sin, 1.0)
    v = z[:, d_attn + d_kv:d_attn + 2 * d_kv]
    q_ref[0] = q.T.astype(bf16)
    vt = v.T.astype(bf16)
    for g in range(d_kv // HEAD_DIM):
        k_ref[0, g] = k[:, g * HEAD_DIM:(g + 1) * HEAD_DIM].astype(bf16)
        for j in range(v_ref.shape[2]):
            v_ref[0, g, j] = vt[g * HEAD_DIM:(g + 1) * HEAD_DIM, j * K_TILE:(j + 1) * K_TILE]
    u_ref[0] = z[:, d_attn + 2 * d_kv:].astype(bf16)


def _inproj(x, shift, scale, gain, w_in, qg, kg, bdq, bdk, cos, sin, d_attn, d_kv):
    b, s, d = x.shape
    d_in = w_in.shape[1]
    d_u = d_in - d_attn - 2 * d_kv
    n_kv = d_kv // HEAD_DIM
    tm = TOKEN_TILE
    const = lambda shape: pl.BlockSpec(shape, lambda bi, i: (0,) * len(shape))
    per_b = pl.BlockSpec((1, 1, d), lambda bi, i: (bi, 0, 0))
    return pl.pallas_call(
        functools.partial(_inproj_kernel, d_attn=d_attn, d_kv=d_kv),
        grid=(b, s // tm),
        in_specs=[pl.BlockSpec((1, tm, d), lambda bi, i: (bi, i, 0)), per_b, per_b,
                  const((1, d)), const((d, d_in)), const((1, d_attn)), const((1, d_kv)),
                  const((d_attn, d_attn)), const((d_kv, d_kv)),
                  pl.BlockSpec((tm, LANES), lambda bi, i: (i, 0)),
                  pl.BlockSpec((tm, LANES), lambda bi, i: (i, 0))],
        out_specs=[pl.BlockSpec((1, d_attn, tm), lambda bi, i: (bi, 0, i)),
                   pl.BlockSpec((1, n_kv, tm, HEAD_DIM), lambda bi, i: (bi, 0, i, 0)),
                   pl.BlockSpec((1, n_kv, tm // K_TILE, HEAD_DIM, K_TILE), lambda bi, i: (bi, 0, i, 0, 0)),
                   pl.BlockSpec((1, tm, d_u), lambda bi, i: (bi, i, 0))],
        out_shape=[jax.ShapeDtypeStruct((b, d_attn, s), bf16),
                   jax.ShapeDtypeStruct((b, n_kv, s, HEAD_DIM), bf16),
                   jax.ShapeDtypeStruct((b, n_kv, s // K_TILE, HEAD_DIM, K_TILE), bf16),
                   jax.ShapeDtypeStruct((b, s, d_u), bf16)],
        compiler_params=_params("arbitrary", "arbitrary"),
        name="inproj",
    )(x, shift, scale, gain, w_in, qg, kg, bdq, bdk, cos, sin)


def _attn_kernel(q_ref, k_ref, v_ref, o_ref, s_scr, q_scr, m_scr, acc_scr, *, n_rep, bounded):
    tq = q_ref.shape[2]
    n_blk, _, tk = v_ref.shape[2:]
    qt = q_ref[0]
    qs = jnp.concatenate([qt[h * HEAD_DIM:(h + 1) * HEAD_DIM, :] for h in range(n_rep)], axis=1)
    cols = qs.shape[1]
    ones = jnp.ones((BF16_SUBLANES, tk), bf16)

    strips = [pl.ds(c, ATTN_STRIP) for c in range(0, cols, ATTN_STRIP)]

    def scores(i, slot):
        kt = k_ref[0, 0, pl.ds(pl.multiple_of(jnp.minimum(i, n_blk - 1) * tk, tk), tk), :]
        for c in strips:
            s_scr[slot, :, c] = jnp.dot(kt, q_scr[:, c], preferred_element_type=f32)

    def attend(i, slot):
        vt = jnp.concatenate([v_ref[0, 0, i], ones], axis=0)
        for c in strips:
            if bounded:
                p = jnp.exp2(s_scr[slot, :, c]).astype(bf16)
                acc_scr[:, c] += jnp.dot(vt, p, preferred_element_type=f32)
            else:
                m = m_scr[:, c]
                m_new = jnp.maximum(m, jnp.max(s_scr[slot, :, c], axis=0, keepdims=True))
                p = jnp.exp2(s_scr[slot, :, c] - m_new).astype(bf16)
                acc_scr[:, c] = jnp.exp2(m - m_new) * acc_scr[:, c] + jnp.dot(vt, p, preferred_element_type=f32)
                m_scr[:, c] = m_new

    per_trip = min(ATTN_BLOCKS_PER_TRIP, n_blk)

    def body(j, _):
        for step in range(per_trip):
            i = j * per_trip + step
            slot = step % 2
            scores(i + 1, 1 - slot)
            attend(i, slot)
        return 0

    q_scr[...] = qs
    m_scr[...] = jnp.full(m_scr.shape, -jnp.inf, f32)
    acc_scr[...] = jnp.zeros(acc_scr.shape, f32)
    scores(0, 0)
    lax.fori_loop(0, n_blk // per_trip, body, 0)
    acc = acc_scr[...]
    o = acc[:HEAD_DIM] / acc[HEAD_DIM:HEAD_DIM + 1]
    o_ref[0] = jnp.concatenate([o[:, h * tq:(h + 1) * tq] for h in range(n_rep)], axis=0).astype(bf16)


def _attention(qt, k, vt, bounded):
    b, d_attn, s = qt.shape
    n_kv, n_blk, _, tk = vt.shape[1:]
    n_rep = d_attn // HEAD_DIM // n_kv
    tq = min(Q_TILE, s)
    gw = n_rep * HEAD_DIM
    return pl.pallas_call(
        functools.partial(_attn_kernel, n_rep=n_rep, bounded=bounded),
        grid=(b, n_kv, s // tq),
        in_specs=[pl.BlockSpec((1, gw, tq), lambda bi, g, i: (bi, g, i)),
                  pl.BlockSpec((1, 1, s, HEAD_DIM), lambda bi, g, i: (bi, g, 0, 0)),
                  pl.BlockSpec((1, 1, n_blk, HEAD_DIM, tk), lambda bi, g, i: (bi, g, 0, 0, 0))],
        out_specs=pl.BlockSpec((1, gw, tq), lambda bi, g, i: (bi, g, i)),
        out_shape=jax.ShapeDtypeStruct((b, d_attn, s), bf16),
        scratch_shapes=[pltpu.VMEM((2, tk, n_rep * tq), f32),
                        pltpu.VMEM((HEAD_DIM, n_rep * tq), bf16), pltpu.VMEM((1, n_rep * tq), f32),
                        pltpu.VMEM((HEAD_DIM + BF16_SUBLANES, n_rep * tq), f32)],
        compiler_params=_params("arbitrary", "arbitrary", "arbitrary"),
        name="attn_bounded" if bounded else "attn",
    )(qt, k, vt)


def _conv_kernel(u_ref, up_ref, un_ref, w_ref, cb_ref, lg_ref, lb_ref, o_ref, a_ref, y_ref, *, ts, dc, width):
    i = pl.program_id(1)
    last = pl.num_programs(1) - 1
    halo = CONV_HALO

    def glu(u):
        u = u.astype(f32)
        return u[:, :dc] * jax.nn.sigmoid(u[:, dc:])

    a_ref[pl.ds(halo, ts), :] = glu(u_ref[0])
    a_ref[pl.ds(0, halo), :] = jnp.where(i > 0, glu(up_ref[0]), 0.0)
    a_ref[pl.ds(halo + ts, halo), :] = jnp.where(i < last, glu(un_ref[0]), 0.0)

    off = halo - width // 2
    sub = 8
    rows = CONV_ROWS
    reach = -(-(off + width) // sub) * sub
    cb = cb_ref[...]
    lg = lg_ref[...]
    lb = lb_ref[...]
    for c in range(ts // rows):
        r0 = c * rows
        for g in range(dc // LANES):
            lanes = pl.ds(g * LANES, LANES)
            blk = a_ref[pl.ds(r0, rows + reach), lanes]
            acc = None
            for r in range(sub):
                part = None
                for u in range(r, off + width, sub):
                    if u < off:
                        continue
                    term = blk[u - r:u - r + rows + sub, :] * w_ref[pl.ds(u - off, 1), lanes]
                    part = term if part is None else part + term
                shifted = part[r:r + rows, :]
                acc = shifted if acc is None else acc + shifted
            y_ref[pl.ds(r0, rows), lanes] = acc
        y = y_ref[pl.ds(r0, rows), :] + cb
        mu = jnp.mean(y, axis=-1, keepdims=True)
        yc = y - mu
        var = jnp.mean(yc * yc, axis=-1, keepdims=True)
        yn = yc * lax.rsqrt(var + EPS) * lg + lb
        o_ref[0, pl.ds(r0, CONV_ROWS), :] = _silu(yn).astype(bf16)


def _conv(u, w, cb, lg, lb):
    b, s, du = u.shape
    dc = du // 2
    width = w.shape[0]
    ts = min(CONV_TILE, s)
    hb = ts // CONV_HALO
    n_h = s // CONV_HALO
    const = lambda shape: pl.BlockSpec(shape, lambda bi, i: (0,) * len(shape))
    return pl.pallas_call(
        functools.partial(_conv_kernel, ts=ts, dc=dc, width=width),
        grid=(b, s // ts),
        in_specs=[pl.BlockSpec((1, ts, du), lambda bi, i: (bi, i, 0)),
                  pl.BlockSpec((1, CONV_HALO, du), lambda bi, i: (bi, jnp.maximum(i * hb - 1, 0), 0)),
                  pl.BlockSpec((1, CONV_HALO, du), lambda bi, i: (bi, jnp.minimum((i + 1) * hb, n_h - 1), 0)),
                  const((width, dc)), const((1, dc)), const((1, dc)), const((1, dc))],
        out_specs=pl.BlockSpec((1, ts, dc), lambda bi, i: (bi, i, 0)),
        out_shape=jax.ShapeDtypeStruct((b, s, dc), bf16),
        scratch_shapes=[pltpu.VMEM((ts + 2 * CONV_HALO, dc), f32), pltpu.VMEM((ts, dc), f32)],
        compiler_params=_params("arbitrary", "arbitrary"),
        name="conv",
    )(u, u, u, w, cb, lg, lb)


def _outproj_kernel(a_ref, c_ref, x_ref, g1_ref, sc_ref, sh_ref, nf_ref, wa_ref, wc_ref, wr1_ref, wr2_ref,
                    x1_ref, h2_ref, aff_ref, *, ne):
    attn = a_ref[0].astype(f32).T.astype(bf16)
    mix = (jnp.dot(attn, wa_ref[...], preferred_element_type=f32)
           + jnp.dot(c_ref[0], wc_ref[...], preferred_element_type=f32))
    x1 = x_ref[0] + g1_ref[0] * mix
    x1_ref[0] = x1
    ms = jnp.mean(x1 * x1, axis=-1, keepdims=True)
    h = x1 * lax.rsqrt(ms + EPS) * nf_ref[...]
    h2 = h * (1.0 + sc_ref[0]) + sh_ref[0]
    hi = h2.astype(bf16)
    lo = (h2 - hi.astype(f32)).astype(bf16)
    h2_ref[0] = hi
    r1 = jnp.dot(hi, wr1_ref[...], preferred_element_type=f32)
    r2 = jnp.dot(lo, wr2_ref[...], preferred_element_type=f32)
    logits = r1[:, :ne] + r1[:, ne:] + r2
    m = jnp.max(logits, axis=-1, keepdims=True)
    e = jnp.exp(logits - m)
    aff_ref[0] = e / jnp.sum(e, axis=-1, keepdims=True)


def _outproj(attn, conv, x, gate1, scale2, shift2, norm_ffn, w_a, w_c, wr1, wr2):
    b, s, d = x.shape
    da = attn.shape[1]
    dc = conv.shape[2]
    ne = wr2.shape[1]
    tm = TOKEN_TILE
    const = lambda shape: pl.BlockSpec(shape, lambda bi, i: (0,) * len(shape))
    per_b = pl.BlockSpec((1, 1, d), lambda bi, i: (bi, 0, 0))
    return pl.pallas_call(
        functools.partial(_outproj_kernel, ne=ne),
        grid=(b, s // tm),
        in_specs=[pl.BlockSpec((1, da, tm), lambda bi, i: (bi, 0, i)),
                  pl.BlockSpec((1, tm, dc), lambda bi, i: (bi, i, 0)),
                  pl.BlockSpec((1, tm, d), lambda bi, i: (bi, i, 0)),
                  per_b, per_b, per_b, const((1, d)), const((da, d)), const((dc, d)),
                  const((d, 2 * ne)), const((d, ne))],
        out_specs=[pl.BlockSpec((1, tm, d), lambda bi, i: (bi, i, 0)),
                   pl.BlockSpec((1, tm, d), lambda bi, i: (bi, i, 0)),
                   pl.BlockSpec((1, tm, ne), lambda bi, i: (bi, i, 0))],
        out_shape=[jax.ShapeDtypeStruct((b, s, d), f32),
                   jax.ShapeDtypeStruct((b, s, d), bf16),
                   jax.ShapeDtypeStruct((b, s, ne), f32)],
        compiler_params=_params("arbitrary", "arbitrary"),
        name="outproj",
    )(attn, conv, x, gate1, scale2, shift2, norm_ffn, w_a, w_c, wr1, wr2)


def _thresh_kernel(bits_ref, thr_ref, ntie_ref, *, cap, ne, chunk):
    rows = bits_ref.shape[0]

    def count_ge(cand):
        cand_row = cand[0:1, :]

        def body(i, acc):
            blk = bits_ref[pl.ds(pl.multiple_of(i * chunk, chunk), chunk), :]
            hit = jnp.where(blk >= cand_row, 1, 0).astype(jnp.int32)
            return acc + jnp.sum(hit.reshape(chunk // 8, 8, LANES), axis=0)
        acc = lax.fori_loop(0, rows // chunk, body, jnp.zeros((8, LANES), jnp.int32))
        tot = jnp.sum(acc, axis=0, keepdims=True)
        tot = jnp.broadcast_to(tot, (8, LANES))
        shift = LANES // 2
        while shift >= ne:
            tot = tot + pltpu.roll(tot, shift, 1)
            shift //= 2
        return tot

    def bit_step(i, prefix):
        cand = prefix | jnp.left_shift(jnp.int32(1), 30 - i)
        return jnp.where(count_ge(cand) >= cap, cand, prefix)

    prefix = lax.fori_loop(0, 31, bit_step, jnp.zeros((8, LANES), jnp.int32))
    thr_ref[...] = prefix
    ntie_ref[...] = cap - count_ge(prefix + 1)


def _thresh(aff_bits, cap, ne):
    rows = aff_bits.shape[0]
    chunk = min(512, rows)
    return pl.pallas_call(
        functools.partial(_thresh_kernel, cap=cap, ne=ne, chunk=chunk),
        out_shape=[jax.ShapeDtypeStruct((8, LANES), jnp.int32)] * 2,
        compiler_params=pltpu.CompilerParams(vmem_limit_bytes=VMEM_LIMIT_BYTES),
        name="thresh",
    )(aff_bits)


def _plan_kernel(aff_ref, thr_ref, ntie_ref, ut_ref, slot_ref, cnt_ref, base_ref, used_ref, base_s, tie_s):
    @pl.when(pl.program_id(0) == 0)
    def _():
        base_s[...] = jnp.zeros_like(base_s)
        tie_s[...] = jnp.zeros_like(tie_s)

    a = aff_ref[...]
    thr = thr_ref[...]
    eq = a == thr
    eqf = jnp.where(eq, 1.0, 0.0)
    ut = ut_ref[...]
    tie_rank = tie_s[...] + jnp.dot(eqf.astype(bf16), ut, preferred_element_type=f32)
    sel = (a > thr) | (eq & (tie_rank < ntie_ref[...]))
    self_ = jnp.where(sel, 1.0, 0.0)
    rank = jnp.dot(self_.astype(bf16), ut, preferred_element_type=f32)
    cnt = jnp.sum(self_, axis=1, keepdims=True)
    base = base_s[...]
    slot_ref[...] = jnp.where(sel, base + rank, -1.0)
    cnt_ref[0] = cnt.astype(jnp.int32)
    base_ref[0] = base.astype(jnp.int32)
    new_base = base + jnp.ceil(cnt * (1.0 / BF16_SUBLANES)) * BF16_SUBLANES
    base_s[...] = new_base
    used_ref[...] = new_base.astype(jnp.int32)
    tie_s[...] = tie_s[...] + jnp.sum(eqf, axis=1, keepdims=True)


def _plan(aff_t, thr, ntie, ut):
    ne, n = aff_t.shape
    t = MOE_TILE
    tiles = n // t
    return pl.pallas_call(
        _plan_kernel,
        grid=(tiles,),
        in_specs=[pl.BlockSpec((ne, t), lambda i: (0, i)),
                  pl.BlockSpec((ne, 1), lambda i: (0, 0)),
                  pl.BlockSpec((ne, 1), lambda i: (0, 0)),
                  pl.BlockSpec((t, t), lambda i: (0, 0))],
        out_specs=[pl.BlockSpec((ne, t), lambda i: (0, i)),
                   pl.BlockSpec((1, ne, 1), lambda i: (i, 0, 0)),
                   pl.BlockSpec((1, ne, 1), lambda i: (i, 0, 0)),
                   pl.BlockSpec((ne, 1), lambda i: (0, 0))],
        out_shape=[jax.ShapeDtypeStruct((ne, n), f32),
                   jax.ShapeDtypeStruct((tiles, ne, 1), jnp.int32),
                   jax.ShapeDtypeStruct((tiles, ne, 1), jnp.int32),
                   jax.ShapeDtypeStruct((ne, 1), jnp.int32)],
        scratch_shapes=[pltpu.VMEM((ne, 1), f32), pltpu.VMEM((ne, 1), f32)],
        compiler_params=_params("arbitrary"),
        name="plan",
    )(aff_t, thr, ntie, ut)


def _blocks_needed(cnt_sm, tile, ne, block):
    mx = cnt_sm[tile * ne]
    for e in range(1, ne):
        mx = jnp.maximum(mx, cnt_sm[tile * ne + e])
    return (mx + block - 1) // block


def _gather_copy(stage, xe_hbm, sem, buf, e, start):
    r = SLOT_BLOCK
    return pltpu.make_async_copy(stage.at[buf, pl.ds(e * r, r), :], xe_hbm.at[e, pl.ds(start, r), :], sem.at[buf])


def _gather_kernel(base_sm, cnt_sm, used_sm, slot_ref, slot_nx, h_ref, xe_hbm, sel_s, stage, zeros_s, sem,
                   *, ne, group):
    j = pl.program_id(0)
    last = pl.num_programs(0) - 1
    n_tiles = 2 * pl.num_programs(0)
    r = SLOT_BLOCK
    gr = GATHER_ROWS
    t = MOE_TILE
    rows_alloc = xe_hbm.shape[1]

    def wait_buf(buf):
        for e in range(ne):
            _gather_copy(stage, xe_hbm, sem, buf, e, 0).wait()

    def start_buf(tile, b, buf):
        for e in range(ne):
            start = pl.multiple_of(base_sm[tile * ne + e] + b * gr, BF16_SUBLANES)
            _gather_copy(stage, xe_hbm, sem, buf, e, start).start()

    def one_hot(tile, b, sref, col0, buf):
        row_iota = lax.broadcasted_iota(jnp.int32, (gr, t), 0).astype(f32)
        for e in range(ne):
            start = (base_sm[tile * ne + e] + b * gr).astype(f32)
            hit = sref[e:e + 1, pl.ds(col0, t)] == row_iota + start
            sel_s[buf, pl.ds(e * gr, gr), :] = jnp.where(hit, 1.0, 0.0).astype(bf16)

    def pick(row0, buf):
        h = h_ref[pl.ds(row0, t), :]
        for c in range(ne // group):
            res = jnp.dot(sel_s[buf, pl.ds(c * group * gr, group * gr), :], h, preferred_element_type=f32)
            for k in range(group):
                stage[buf, pl.ds((c * group + k) * r, gr), :] = res[k * gr:(k + 1) * gr].astype(bf16)

    def blocks(tile):
        return _blocks_needed(cnt_sm, tile, ne, gr)

    def extra_blocks(tile, row0, buf):
        n = blocks(tile)

        @pl.when(n > 1)
        def _():
            wait_buf(buf)

        def extra(b, _):
            one_hot(tile, b, slot_ref, row0, 2)
            pick(row0, 2)
            start_buf(tile, b, 2)
            wait_buf(2)
            return 0

        lax.fori_loop(1, jnp.maximum(n, 1), extra, 0)

    t0 = 2 * j
    t1 = t0 + 1
    t2 = jnp.minimum(t0 + 2, n_tiles - 1)

    @pl.when(j == 0)
    def _():
        stage[...] = jnp.zeros(stage.shape, bf16)
        one_hot(0, 0, slot_ref, 0, 0)

    pick(0, 0)
    one_hot(t1, 0, slot_ref, t, 1)

    @pl.when((j > 0) & (blocks(jnp.maximum(t0 - 1, 0)) <= 1))
    def _():
        wait_buf(1)

    start_buf(t0, 0, 0)
    extra_blocks(t0, 0, 0)

    pick(t, 1)
    one_hot(t2, 0, slot_nx, 0, 0)

    @pl.when(blocks(t0) <= 1)
    def _():
        wait_buf(0)

    start_buf(t1, 0, 1)
    extra_blocks(t1, t, 1)

    @pl.when((j == last) & (blocks(t1) <= 1))
    def _():
        wait_buf(1)

    @pl.when(j == last)
    def _():
        zeros_s[...] = jnp.zeros_like(zeros_s)

        def fill_copy(e, start, rows):
            return pltpu.make_async_copy(zeros_s.at[pl.ds(0, rows), :], xe_hbm.at[e, pl.ds(start, rows), :],
                                         sem.at[3])

        small = BF16_SUBLANES
        for e in range(ne):
            used = used_sm[e]
            n_big = (rows_alloc - used) // r
            n_small = (rows_alloc - used - n_big * r) // small

            def fill_big(k, _, e=e, used=used):
                fill_copy(e, pl.multiple_of(used + k * r, small), r).start()
                return 0

            def fill_small(k, _, e=e, used=used, n_big=n_big):
                fill_copy(e, pl.multiple_of(used + n_big * r + k * small, small), small).start()
                return 0

            def drain_big(k, _, e=e):
                fill_copy(e, 0, r).wait()
                return 0

            def drain_small(k, _, e=e):
                fill_copy(e, 0, small).wait()
                return 0

            lax.fori_loop(0, n_big, fill_big, 0)
            lax.fori_loop(0, n_small, fill_small, 0)
            lax.fori_loop(0, n_big, drain_big, 0)
            lax.fori_loop(0, n_small, drain_small, 0)


def _gather(base, cnt, used, slot_t, h2, rows_alloc):
    ne, n = slot_t.shape
    d = h2.shape[1]
    t = MOE_TILE
    tiles = n // t
    group = 4
    return pl.pallas_call(
        functools.partial(_gather_kernel, ne=ne, group=group),
        grid_spec=pltpu.PrefetchScalarGridSpec(
            num_scalar_prefetch=3,
            grid=(tiles // 2,),
            in_specs=[pl.BlockSpec((ne, 2 * t), lambda j, *_: (0, j)),
                      pl.BlockSpec((ne, t), lambda j, *_: (0, jnp.minimum(2 * j + 2, tiles - 1))),
                      pl.BlockSpec((2 * t, d), lambda j, *_: (j, 0))],
            out_specs=pl.BlockSpec(memory_space=pl.ANY),
            scratch_shapes=[pltpu.VMEM((3, ne * GATHER_ROWS, t), bf16),
                            pltpu.VMEM((3, ne * SLOT_BLOCK, d), bf16),
                            pltpu.VMEM((SLOT_BLOCK, d), bf16),
                            pltpu.SemaphoreType.DMA((4,))]),
        out_shape=jax.ShapeDtypeStruct((ne, rows_alloc, d), bf16),
        compiler_params=_params("arbitrary"),
        name="gather",
    )(base, cnt, used, slot_t, slot_t, h2)


def _ffn_kernel(used_sm, x_ref, wg_ref, wu_ref, wd_ref, o_ref, wg_s, wu_s, wd_s):
    e = pl.program_id(0)
    tm = x_ref.shape[1]
    r0 = pl.program_id(1) * tm
    used = used_sm[e]

    @pl.when(pl.program_id(1) == 0)
    def _():
        wg_s[...] = wg_ref[0].astype(bf16)
        wu_s[...] = wu_ref[0].astype(bf16)
        wd_s[...] = wd_ref[0].astype(bf16)

    @pl.when(r0 < used)
    def _():
        x = x_ref[0]
        a = jnp.dot(x, wg_s[...], preferred_element_type=f32)
        b = jnp.dot(x, wu_s[...], preferred_element_type=f32)
        hid = (_silu(a) * b).astype(bf16)
        o_ref[0] = jnp.dot(hid, wd_s[...], preferred_element_type=f32).astype(bf16)

    @pl.when(r0 >= used)
    def _():
        o_ref[...] = jnp.zeros_like(o_ref)


def _ffn(used, xe, wg, wu, wd):
    ne, rows, d = xe.shape
    f = wg.shape[2]
    tm = FFN_TILE
    return pl.pallas_call(
        _ffn_kernel,
        grid_spec=pltpu.PrefetchScalarGridSpec(
            num_scalar_prefetch=1,
            grid=(ne, rows // tm),
            in_specs=[pl.BlockSpec((1, tm, d), lambda e, i, *_: (e, i, 0)),
                      pl.BlockSpec((1, d, f), lambda e, i, *_: (e, 0, 0)),
                      pl.BlockSpec((1, d, f), lambda e, i, *_: (e, 0, 0)),
                      pl.BlockSpec((1, f, d), lambda e, i, *_: (e, 0, 0))],
            out_specs=pl.BlockSpec((1, tm, d), lambda e, i, *_: (e, i, 0)),
            scratch_shapes=[pltpu.VMEM((d, f), bf16), pltpu.VMEM((d, f), bf16), pltpu.VMEM((f, d), bf16)]),
        out_shape=jax.ShapeDtypeStruct((ne, rows, d), bf16),
        compiler_params=_params("arbitrary", "arbitrary"),
        name="ffn",
    )(used, xe, wg, wu, wd)


def _combine_copy(ye_hbm, ybuf, sem, buf, e, start):
    r = SLOT_BLOCK
    return pltpu.make_async_copy(ye_hbm.at[e, pl.ds(start, r), :], ybuf.at[buf, pl.ds(e * SLOT_LANES, r), :],
                                 sem.at[buf])


def _combine_kernel(base_sm, cnt_sm, slot_ref, aff_ref, slot_nx, aff_nx, x1_ref, g2_ref, ye_hbm, o_ref,
                    sel_s, ybuf, sem, *, ne):
    j = pl.program_id(0)
    last = pl.num_programs(0) - 1
    r = SLOT_BLOCK
    t = MOE_TILE
    n_tiles = 2 * pl.num_programs(0)

    def start_block(tile, b, buf):
        for e in range(ne):
            start = pl.multiple_of(base_sm[tile * ne + e] + b * r, BF16_SUBLANES)
            _combine_copy(ye_hbm, ybuf, sem, buf, e, start).start()

    def wait_block(buf):
        for e in range(ne):
            _combine_copy(ye_hbm, ybuf, sem, buf, e, 0).wait()

    def one_hot(tile, b, sref, aref, row0, buf):
        lane = lax.broadcasted_iota(jnp.int32, (t, SLOT_LANES), 1)
        lane_iota = jnp.where(lane < r, lane, -(1 << 24)).astype(f32)
        rows = pl.ds(row0, t)
        for e in range(ne):
            start = (base_sm[tile * ne + e] + b * r).astype(f32)
            hit = sref[rows, e:e + 1] == lane_iota + start
            sel_s[buf, :, pl.ds(e * SLOT_LANES, SLOT_LANES)] = jnp.where(hit, aref[rows, e:e + 1], 0.0).astype(bf16)

    def finish(row0, buf):
        rows = pl.ds(row0, t)
        o_ref[rows, :] = x1_ref[rows, :] + g2_ref[0] * jnp.dot(sel_s[buf], ybuf[buf], preferred_element_type=f32)

    def extra_blocks(tile, row0):
        rows = pl.ds(row0, t)

        def extra(b, _):
            start_block(tile, b, 2)
            one_hot(tile, b, slot_ref, aff_ref, row0, 2)
            wait_block(2)
            o_ref[rows, :] += g2_ref[0] * jnp.dot(sel_s[2], ybuf[2], preferred_element_type=f32)
            return 0

        lax.fori_loop(1, jnp.maximum(_blocks_needed(cnt_sm, tile, ne, r), 1), extra, 0)

    t0 = 2 * j
    t1 = t0 + 1
    t2 = jnp.minimum(t0 + 2, n_tiles - 1)

    @pl.when(j == 0)
    def _():
        ybuf[...] = jnp.zeros(ybuf.shape, bf16)
        one_hot(0, 0, slot_ref, aff_ref, 0, 0)
        start_block(0, 0, 0)

    start_block(t1, 0, 1)
    wait_block(0)
    finish(0, 0)
    one_hot(t1, 0, slot_ref, aff_ref, t, 1)
    extra_blocks(t0, 0)

    @pl.when(j < last)
    def _():
        start_block(t2, 0, 0)

    wait_block(1)
    finish(t, 1)
    one_hot(t2, 0, slot_nx, aff_nx, 0, 0)
    extra_blocks(t1, t)


def _combine(base, cnt, slot, aff, x1, gate2, ye, seq):
    n, ne = slot.shape
    d = x1.shape[1]
    t = MOE_TILE
    tiles = n // t
    per_seq = seq // (2 * t)
    nxt = lambda j, *_: (jnp.minimum(2 * j + 2, tiles - 1), 0)
    return pl.pallas_call(
        functools.partial(_combine_kernel, ne=ne),
        grid_spec=pltpu.PrefetchScalarGridSpec(
            num_scalar_prefetch=2,
            grid=(tiles // 2,),
            in_specs=[pl.BlockSpec((2 * t, ne), lambda j, *_: (j, 0)),
                      pl.BlockSpec((2 * t, ne), lambda j, *_: (j, 0)),
                      pl.BlockSpec((t, ne), nxt),
                      pl.BlockSpec((t, ne), nxt),
                      pl.BlockSpec((2 * t, d), lambda j, *_: (j, 0)),
                      pl.BlockSpec((1, 1, d), lambda j, *_: (j // per_seq, 0, 0)),
                      pl.BlockSpec(memory_space=pl.ANY)],
            out_specs=pl.BlockSpec((2 * t, d), lambda j, *_: (j, 0)),
            scratch_shapes=[pltpu.VMEM((3, t, ne * SLOT_LANES), bf16),
                            pltpu.VMEM((3, ne * SLOT_LANES, d), bf16),
                            pltpu.SemaphoreType.DMA((3,))]),
        out_shape=jax.ShapeDtypeStruct((n, d), f32),
        compiler_params=_params("arbitrary"),
        name="combine",
    )(base, cnt, slot, aff, slot, aff, x1, gate2, ye)


def _rope_tables(s):
    t = jnp.arange(s)
    row = (t // GRID_W).astype(f32)
    col = (t % GRID_W).astype(f32)
    axis_dim = HEAD_DIM // 2
    inv = ROPE_THETA ** (-jnp.arange(0, axis_dim, 2, dtype=f32) / axis_dim)
    ang = jnp.concatenate([row[:, None] * inv, col[:, None] * inv], axis=-1)
    cos, sin = jnp.cos(ang), jnp.sin(ang)
    reps = LANES // HEAD_DIM
    cos_t = jnp.tile(jnp.concatenate([cos, cos], axis=-1), (1, reps))
    sin_t = jnp.tile(jnp.concatenate([-sin, sin], axis=-1), (1, reps))
    return cos_t, sin_t


def _head_perm(n_heads):
    half = jnp.concatenate([jnp.arange(0, HEAD_DIM, 2), jnp.arange(1, HEAD_DIM, 2)])
    return (jnp.arange(n_heads)[:, None] * HEAD_DIM + half[None, :]).reshape(-1)


def _block_mean(width):
    idx = jnp.arange(width) // HEAD_DIM
    return jnp.where(idx[:, None] == idx[None, :], 1.0 / HEAD_DIM, 0.0).astype(bf16)


def _prepare(norm_mix, w_in, q_norm, k_norm, conv_w, conv_b, conv_ln_g, conv_ln_b, w_out, norm_ffn,
             w_router, w_gate, w_up, w_down):
    d = w_in.shape[0]
    d_conv = conv_w.shape[-1]
    d_attn = w_out.shape[0] - d_conv
    d_kv = (w_in.shape[1] - d_attn - 2 * d_conv) // 2
    n_heads = d_attn // HEAD_DIM
    n_kv = d_kv // HEAD_DIM
    qp = _head_perm(n_heads)
    kp = _head_perm(n_kv)
    cols = jnp.concatenate([qp, d_attn + kp, jnp.arange(d_attn + d_kv, w_in.shape[1])])
    wr_hi = w_router.astype(bf16)
    wr_lo = (w_router - wr_hi.astype(f32)).astype(bf16)
    score_bound = HEAD_DIM * (HEAD_DIM ** -0.5 * LOG2E) * jnp.max(jnp.abs(q_norm)) * jnp.max(jnp.abs(k_norm))
    return dict(
        d_attn=d_attn, d_kv=d_kv, score_bound=score_bound,
        norm_mix=norm_mix.reshape(1, d),
        w_in=w_in[:, cols].astype(bf16),
        qg=jnp.tile(q_norm[_head_perm(1)], n_heads).reshape(1, d_attn),
        kg=jnp.tile(k_norm[_head_perm(1)], n_kv).reshape(1, d_kv),
        bdq=_block_mean(d_attn), bdk=_block_mean(d_kv),
        conv_w=conv_w.reshape(conv_w.shape[0], d_conv), conv_b=conv_b.reshape(1, d_conv),
        conv_ln_g=conv_ln_g.reshape(1, d_conv), conv_ln_b=conv_ln_b.reshape(1, d_conv),
        w_a=w_out[:d_attn].astype(bf16), w_c=w_out[d_attn:].astype(bf16),
        norm_ffn=norm_ffn.reshape(1, d),
        wr1=jnp.concatenate([wr_hi, wr_lo], axis=1), wr2=wr_hi,
        w_gate=w_gate, w_up=w_up, w_down=w_down,
    )


def _encoder_layer(x, mod, p):
    b, s, d = x.shape
    shift1, scale1, gate1, shift2, scale2, gate2 = [m.reshape(b, 1, d) for m in jnp.split(mod, 6, axis=-1)]
    cos, sin = _rope_tables(s)
    q, k, v, u = _inproj(x, shift1, scale1, p["norm_mix"], p["w_in"], p["qg"], p["kg"], p["bdq"], p["bdk"],
                         cos, sin, p["d_attn"], p["d_kv"])
    attn = lax.cond(p["score_bound"] <= SCORE_BOUND_LOG2,
                    lambda: _attention(q, k, v, True), lambda: _attention(q, k, v, False))
    conv = _conv(u, p["conv_w"], p["conv_b"], p["conv_ln_g"], p["conv_ln_b"])
    x1, h2, aff = _outproj(attn, conv, x, gate1, scale2, shift2, p["norm_ffn"], p["w_a"], p["w_c"],
                           p["wr1"], p["wr2"])

    n = b * s
    ne = aff.shape[-1]
    cap = CAPACITY_FACTOR * n // ne
    aff = aff.reshape(n, ne)
    bits = lax.bitcast_convert_type(aff, jnp.int32).reshape(n * ne // LANES, LANES)
    thr_bits, ntie = _thresh(bits, cap, ne)
    thr = lax.bitcast_convert_type(thr_bits[0, :ne], f32).reshape(ne, 1)
    ntie = ntie[0, :ne].astype(f32).reshape(ne, 1)
    t = MOE_TILE
    tok = jnp.arange(t)
    ut = (tok[:, None] < tok[None, :]).astype(bf16)
    slot_t, cnt, base, used = _plan(aff.T, thr, ntie, ut)
    cnt = cnt.reshape(-1)
    base = base.reshape(-1)
    tiles = n // t
    rows_alloc = -(-(cap + BF16_SUBLANES * tiles + SLOT_BLOCK) // FFN_TILE) * FFN_TILE
    used = used.reshape(-1)
    xe = _gather(base, cnt, used, slot_t, h2.reshape(n, d), rows_alloc)
    ye = _ffn(used, xe, p["w_gate"], p["w_up"], p["w_down"])
    out = _combine(base, cnt, slot_t.T, aff, x1.reshape(n, d), gate2, ye, s)
    return out.reshape(b, s, d)


def kernel(x_prompt, x_sample, c_prompt, c_sample, ada_w, ada_b, norm_mix, w_in, q_norm, k_norm, conv_w, conv_b,
           conv_ln_g, conv_ln_b, w_out, norm_ffn, w_router, w_gate, w_up, w_down):
    y_prompt, y_sample = x_prompt, x_sample
    nb = x_prompt.shape[0]
    for l in range(ada_w.shape[0]):
        p = _prepare(norm_mix[l], w_in[l], q_norm[l], k_norm[l], conv_w[l], conv_b[l], conv_ln_g[l],
                     conv_ln_b[l], w_out[l], norm_ffn[l], w_router[l], w_gate[l], w_up[l], w_down[l])
        mod = _ada(jnp.concatenate([c_prompt, c_sample], axis=0), ada_w[l], ada_b[l])
        y_prompt = _encoder_layer(y_prompt, mod[:nb], p)
        y_sample = _encoder_layer(y_sample, mod[nb:], p)
    return (y_prompt, y_sample)
```

```python
import functools

import jax
import jax.numpy as jnp
from jax import lax
from jax.experimental import pallas as pl
from jax.experimental.pallas import tpu as pltpu

HEAD_DIM = 64
N_KV_HEADS = 2
GRID_W = 64
ROPE_THETA = 10000.0
CAPACITY_FACTOR = 2
EPS = 1e-6
LOG2E = 1.4426950408889634

LANES = 128
BF16_SUBLANES = 16
VMEM_LIMIT_BYTES = 48 * 1024 * 1024

TOKEN_TILE = 512
MOE_TILE = 512
SLOT_BLOCK = 128
SLOT_LANES = LANES
FFN_TILE = 512
Q_TILE = 256
K_TILE = 256
ATTN_STRIP = 256
SCORE_BOUND_LOG2 = 56.0
ATTN_BLOCKS_PER_TRIP = 16
CONV_TILE = 512
CONV_HALO = 16
CONV_ROWS = 64

f32 = jnp.float32
bf16 = jnp.bfloat16


def _params(*sem):
    return pltpu.CompilerParams(dimension_semantics=sem, vmem_limit_bytes=VMEM_LIMIT_BYTES)


def _silu(x):
    return x * jax.nn.sigmoid(x)


def _ada_kernel(c_ref, w_ref, b_ref, o_ref):
    c = c_ref[...]
    o_ref[...] = jnp.dot(_silu(c), w_ref[...], preferred_element_type=f32,
                         precision=lax.Precision.HIGHEST) + b_ref[...]


def _ada(c, w, b):
    r, d = c.shape
    n = w.shape[1]
    tn = n // 4
    return pl.pallas_call(
        _ada_kernel,
        grid=(n // tn,),
        in_specs=[pl.BlockSpec((r, d), lambda j: (0, 0)),
                  pl.BlockSpec((d, tn), lambda j: (0, j)),
                  pl.BlockSpec((1, tn), lambda j: (0, j))],
        out_specs=pl.BlockSpec((r, tn), lambda j: (0, j)),
        out_shape=jax.ShapeDtypeStruct((r, n), f32),
        compiler_params=_params("arbitrary"),
        name="ada",
    )(c, w, b.reshape(1, n))


def _norm_rope(t, bd_ref, gain_ref, cos, sin, scale):
    ms = jnp.dot((t * t).astype(bf16), bd_ref[...], preferred_element_type=f32)
    tn = t * lax.rsqrt(ms + EPS) * gain_ref[...]
    w = t.shape[1]
    reps = w // LANES
    if reps > 1:
        cos = jnp.concatenate([cos] * reps, axis=1)
        sin = jnp.concatenate([sin] * reps, axis=1)
    lane = lax.broadcasted_iota(jnp.int32, tn.shape, 1)
    first_half = (lane % HEAD_DIM) < (HEAD_DIM // 2)
    partner = jnp.where(first_half, pltpu.roll(tn, w - HEAD_DIM // 2, 1), pltpu.roll(tn, HEAD_DIM // 2, 1))
    return (tn * cos + partner * sin) * scale


def _inproj_kernel(x_ref, sh_ref, sc_ref, g_ref, w_ref, qg_ref, kg_ref, bdq_ref, bdk_ref, cos_ref, sin_ref,
                   q_ref, k_ref, v_ref, u_ref, *, d_attn, d_kv):
    x = x_ref[0]
    ms = jnp.mean(x * x, axis=-1, keepdims=True)
    h = x * lax.rsqrt(ms + EPS) * g_ref[...]
    h = h * (1.0 + sc_ref[0]) + sh_ref[0]
    z = jnp.dot(h.astype(bf16), w_ref[...], preferred_element_type=f32)
    cos = cos_ref[...]
    sin = sin_ref[...]
    q = _norm_rope(z[:, :d_attn], bdq_ref, qg_ref, cos, sin, HEAD_DIM ** -0.5 * LOG2E)
    k = _norm_rope(z[:, d_attn:d_attn + d_kv], bdk_ref, kg_ref, cos, sin, 1.0)
    v = z[:, d_attn + d_kv:d_attn + 2 * d_kv]
    q_ref[0] = q.T.astype(bf16)
    vt = v.T.astype(bf16)
    for g in range(d_kv // HEAD_DIM):
        k_ref[0, g] = k[:, g * HEAD_DIM:(g + 1) * HEAD_DIM].astype(bf16)
        for j in range(v_ref.shape[2]):
            v_ref[0, g, j] = vt[g * HEAD_DIM:(g + 1) * HEAD_DIM, j * K_TILE:(j + 1) * K_TILE]
    u_ref[0] = z[:, d_attn + 2 * d_kv:].astype(bf16)


def _inproj(x, shift, scale, gain, w_in, qg, kg, bdq, bdk, cos, sin, d_attn, d_kv):
    b, s, d = x.shape
    d_in = w_in.shape[1]
    d_u = d_in - d_attn - 2 * d_kv
    n_kv = d_kv // HEAD_DIM
    tm = TOKEN_TILE
    const = lambda shape: pl.BlockSpec(shape, lambda bi, i: (0,) * len(shape))
    per_b = pl.BlockSpec((1, 1, d), lambda bi, i: (bi, 0, 0))
    return pl.pallas_call(
        functools.partial(_inproj_kernel, d_attn=d_attn, d_kv=d_kv),
        grid=(b, s // tm),
        in_specs=[pl.BlockSpec((1, tm, d), lambda bi, i: (bi, i, 0)), per_b, per_b,
                  const((1, d)), const((d, d_in)), const((1, d_attn)), const((1, d_kv)),
                  const((d_attn, d_attn)), const((d_kv, d_kv)),
                  pl.BlockSpec((tm, LANES), lambda bi, i: (i, 0)),
                  pl.BlockSpec((tm, LANES), lambda bi, i: (i, 0))],
        out_specs=[pl.BlockSpec((1, d_attn, tm), lambda bi, i: (bi, 0, i)),
                   pl.BlockSpec((1, n_kv, tm, HEAD_DIM), lambda bi, i: (bi, 0, i, 0)),
                   pl.BlockSpec((1, n_kv, tm // K_TILE, HEAD_DIM, K_TILE), lambda bi, i: (bi, 0, i, 0, 0)),
                   pl.BlockSpec((1, tm, d_u), lambda bi, i: (bi, i, 0))],
        out_shape=[jax.ShapeDtypeStruct((b, d_attn, s), bf16),
                   jax.ShapeDtypeStruct((b, n_kv, s, HEAD_DIM), bf16),
                   jax.ShapeDtypeStruct((b, n_kv, s // K_TILE, HEAD_DIM, K_TILE), bf16),
                   jax.ShapeDtypeStruct((b, s, d_u), bf16)],
        compiler_params=_params("arbitrary", "arbitrary"),
        name="inproj",
    )(x, shift, scale, gain, w_in, qg, kg, bdq, bdk, cos, sin)


def _attn_kernel(q_ref, k_ref, v_ref, o_ref, s_scr, q_scr, m_scr, acc_scr, *, n_rep, bounded):
    tq = q_ref.shape[2]
    n_blk, _, tk = v_ref.shape[2:]
    qt = q_ref[0]
    qs = jnp.concatenate([qt[h * HEAD_DIM:(h + 1) * HEAD_DIM, :] for h in range(n_rep)], axis=1)
    cols = qs.shape[1]
    ones = jnp.ones((BF16_SUBLANES, tk), bf16)

    strips = [pl.ds(c, ATTN_STRIP) for c in range(0, cols, ATTN_STRIP)]

    def scores(i, slot):
        kt = k_ref[0, 0, pl.ds(pl.multiple_of(jnp.minimum(i, n_blk - 1) * tk, tk), tk), :]
        for c in strips:
            s_scr[slot, :, c] = jnp.dot(kt, q_scr[:, c], preferred_element_type=f32)

    def attend(i, slot):
        vt = jnp.concatenate([v_ref[0, 0, i], ones], axis=0)
        for c in strips:
            if bounded:
                p = jnp.exp2(s_scr[slot, :, c]).astype(bf16)
                acc_scr[:, c] += jnp.dot(vt, p, preferred_element_type=f32)
            else:
                m = m_scr[:, c]
                m_new = jnp.maximum(m, jnp.max(s_scr[slot, :, c], axis=0, keepdims=True))
                p = jnp.exp2(s_scr[slot, :, c] - m_new).astype(bf16)
                acc_scr[:, c] = jnp.exp2(m - m_new) * acc_scr[:, c] + jnp.dot(vt, p, preferred_element_type=f32)
                m_scr[:, c] = m_new

    per_trip = min(ATTN_BLOCKS_PER_TRIP, n_blk)

    def body(j, _):
        for step in range(per_trip):
            i = j * per_trip + step
            slot = step % 2
            scores(i + 1, 1 - slot)
            attend(i, slot)
        return 0

    q_scr[...] = qs
    m_scr[...] = jnp.full(m_scr.shape, -jnp.inf, f32)
    acc_scr[...] = jnp.zeros(acc_scr.shape, f32)
    scores(0, 0)
    lax.fori_loop(0, n_blk // per_trip, body, 0)
    acc = acc_scr[...]
    o = acc[:HEAD_DIM] / acc[HEAD_DIM:HEAD_DIM + 1]
    o_ref[0] = jnp.concatenate([o[:, h * tq:(h + 1) * tq] for h in range(n_rep)], axis=0).astype(bf16)


def _attention(qt, k, vt, bounded):
    b, d_attn, s = qt.shape
    n_kv, n_blk, _, tk = vt.shape[1:]
    n_rep = d_attn // HEAD_DIM // n_kv
    tq = min(Q_TILE, s)
    gw = n_rep * HEAD_DIM
    return pl.pallas_call(
        functools.partial(_attn_kernel, n_rep=n_rep, bounded=bounded),
        grid=(b, n_kv, s // tq),
        in_specs=[pl.BlockSpec((1, gw, tq), lambda bi, g, i: (bi, g, i)),
                  pl.BlockSpec((1, 1, s, HEAD_DIM), lambda bi, g, i: (bi, g, 0, 0)),
                  pl.BlockSpec((1, 1, n_blk, HEAD_DIM, tk), lambda bi, g, i: (bi, g, 0, 0, 0))],
        out_specs=pl.BlockSpec((1, gw, tq), lambda bi, g, i: (bi, g, i)),
        out_shape=jax.ShapeDtypeStruct((b, d_attn, s), bf16),
        scratch_shapes=[pltpu.VMEM((2, tk, n_rep * tq), f32),
                        pltpu.VMEM((HEAD_DIM, n_rep * tq), bf16), pltpu.VMEM((1, n_rep * tq), f32),
                        pltpu.VMEM((HEAD_DIM + BF16_SUBLANES, n_rep * tq), f32)],
        compiler_params=_params("arbitrary", "arbitrary", "arbitrary"),
        name="attn_bounded" if bounded else "attn",
    )(qt, k, vt)


def _conv_kernel(u_ref, up_ref, un_ref, w_ref, cb_ref, lg_ref, lb_ref, o_ref, a_ref, y_ref, *, ts, dc, width):
    i = pl.program_id(1)
    last = pl.num_programs(1) - 1
    halo = CONV_HALO

    def glu(u):
        u = u.astype(f32)
        return u[:, :dc] * jax.nn.sigmoid(u[:, dc:])

    a_ref[pl.ds(halo, ts), :] = glu(u_ref[0])
    a_ref[pl.ds(0, halo), :] = jnp.where(i > 0, glu(up_ref[0]), 0.0)
    a_ref[pl.ds(halo + ts, halo), :] = jnp.where(i < last, glu(un_ref[0]), 0.0)

    off = halo - width // 2
    sub = 8
    rows = CONV_ROWS
    reach = -(-(off + width) // sub) * sub
    cb = cb_ref[...]
    lg = lg_ref[...]
    lb = lb_ref[...]
    for c in range(ts // rows):
        r0 = c * rows
        for g in range(dc // LANES):
            lanes = pl.ds(g * LANES, LANES)
            blk = a_ref[pl.ds(r0, rows + reach), lanes]
            acc = None
            for r in range(sub):
                part = None
                for u in range(r, off + width, sub):
                    if u < off:
                        continue
                    term = blk[u - r:u - r + rows + sub, :] * w_ref[pl.ds(u - off, 1), lanes]
                    part = term if part is None else part + term
                shifted = part[r:r + rows, :]
                acc = shifted if acc is None else acc + shifted
            y_ref[pl.ds(r0, rows), lanes] = acc
        y = y_ref[pl.ds(r0, rows), :] + cb
        mu = jnp.mean(y, axis=-1, keepdims=True)
        yc = y - mu
        var = jnp.mean(yc * yc, axis=-1, keepdims=True)
        yn = yc * lax.rsqrt(var + EPS) * lg + lb
        o_ref[0, pl.ds(r0, CONV_ROWS), :] = _silu(yn).astype(bf16)


def _conv(u, w, cb, lg, lb):
    b, s, du = u.shape
    dc = du // 2
    width = w.shape[0]
    ts = min(CONV_TILE, s)
    hb = ts // CONV_HALO
    n_h = s // CONV_HALO
    const = lambda shape: pl.BlockSpec(shape, lambda bi, i: (0,) * len(shape))
    return pl.pallas_call(
        functools.partial(_conv_kernel, ts=ts, dc=dc, width=width),
        grid=(b, s // ts),
        in_specs=[pl.BlockSpec((1, ts, du), lambda bi, i: (bi, i, 0)),
                  pl.BlockSpec((1, CONV_HALO, du), lambda bi, i: (bi, jnp.maximum(i * hb - 1, 0), 0)),
                  pl.BlockSpec((1, CONV_HALO, du), lambda bi, i: (bi, jnp.minimum((i + 1) * hb, n_h - 1), 0)),
                  const((width, dc)), const((1, dc)), const((1, dc)), const((1, dc))],
        out_specs=pl.BlockSpec((1, ts, dc), lambda bi, i: (bi, i, 0)),
        out_shape=jax.ShapeDtypeStruct((b, s, dc), bf16),
        scratch_shapes=[pltpu.VMEM((ts + 2 * CONV_HALO, dc), f32), pltpu.VMEM((ts, dc), f32)],
        compiler_params=_params("arbitrary", "arbitrary"),
        name="conv",
    )(u, u, u, w, cb, lg, lb)


def _outproj_kernel(a_ref, c_ref, x_ref, g1_ref, sc_ref, sh_ref, nf_ref, wa_ref, wc_ref, wr1_ref, wr2_ref,
                    x1_ref, h2_ref, aff_ref, *, ne):
    attn = a_ref[0].astype(f32).T.astype(bf16)
    mix = (jnp.dot(attn, wa_ref[...], preferred_element_type=f32)
           + jnp.dot(c_ref[0], wc_ref[...], preferred_element_type=f32))
    x1 = x_ref[0] + g1_ref[0] * mix
    x1_ref[0] = x1
    ms = jnp.mean(x1 * x1, axis=-1, keepdims=True)
    h = x1 * lax.rsqrt(ms + EPS) * nf_ref[...]
    h2 = h * (1.0 + sc_ref[0]) + sh_ref[0]
    hi = h2.astype(bf16)
    lo = (h2 - hi.astype(f32)).astype(bf16)
    h2_ref[0] = hi
    r1 = jnp.dot(hi, wr1_ref[...], preferred_element_type=f32)
    r2 = jnp.dot(lo, wr2_ref[...], preferred_element_type=f32)
    logits = r1[:, :ne] + r1[:, ne:] + r2
    m = jnp.max(logits, axis=-1, keepdims=True)
    e = jnp.exp(logits - m)
    aff_ref[0] = e / jnp.sum(e, axis=-1, keepdims=True)


def _outproj(attn, conv, x, gate1, scale2, shift2, norm_ffn, w_a, w_c, wr1, wr2):
    b, s, d = x.shape
    da = attn.shape[1]
    dc = conv.shape[2]
    ne = wr2.shape[1]
    tm = TOKEN_TILE
    const = lambda shape: pl.BlockSpec(shape, lambda bi, i: (0,) * len(shape))
    per_b = pl.BlockSpec((1, 1, d), lambda bi, i: (bi, 0, 0))
    return pl.pallas_call(
        functools.partial(_outproj_kernel, ne=ne),
        grid=(b, s // tm),
        in_specs=[pl.BlockSpec((1, da, tm), lambda bi, i: (bi, 0, i)),
                  pl.BlockSpec((1, tm, dc), lambda bi, i: (bi, i, 0)),
                  pl.BlockSpec((1, tm, d), lambda bi, i: (bi, i, 0)),
                  per_b, per_b, per_b, const((1, d)), const((da, d)), const((dc, d)),
                  const((d, 2 * ne)), const((d, ne))],
        out_specs=[pl.BlockSpec((1, tm, d), lambda bi, i: (bi, i, 0)),
                   pl.BlockSpec((1, tm, d), lambda bi, i: (bi, i, 0)),
                   pl.BlockSpec((1, tm, ne), lambda bi, i: (bi, i, 0))],
        out_shape=[jax.ShapeDtypeStruct((b, s, d), f32),
                   jax.ShapeDtypeStruct((b, s, d), bf16),
                   jax.ShapeDtypeStruct((b, s, ne), f32)],
        compiler_params=_params("arbitrary", "arbitrary"),
        name="outproj",
    )(attn, conv, x, gate1, scale2, shift2, norm_ffn, w_a, w_c, wr1, wr2)


def _thresh_kernel(bits_ref, thr_ref, ntie_ref, *, cap, ne, chunk):
    rows = bits_ref.shape[0]

    def count_ge(cand):
        cand_row = cand[0:1, :]

        def body(i, acc):
            blk = bits_ref[pl.ds(pl.multiple_of(i * chunk, chunk), chunk), :]
            hit = jnp.where(blk >= cand_row, 1, 0).astype(jnp.int32)
            return acc + jnp.sum(hit.reshape(chunk // 8, 8, LANES), axis=0)
        acc = lax.fori_loop(0, rows // chunk, body, jnp.zeros((8, LANES), jnp.int32))
        tot = jnp.sum(acc, axis=0, keepdims=True)
        tot = jnp.broadcast_to(tot, (8, LANES))
        shift = LANES // 2
        while shift >= ne:
            tot = tot + pltpu.roll(tot, shift, 1)
            shift //= 2
        return tot

    def bit_step(i, prefix):
        cand = prefix | jnp.left_shift(jnp.int32(1), 30 - i)
        return jnp.where(count_ge(cand) >= cap, cand, prefix)

    prefix = lax.fori_loop(0, 31, bit_step, jnp.zeros((8, LANES), jnp.int32))
    thr_ref[...] = prefix
    ntie_ref[...] = cap - count_ge(prefix + 1)


def _thresh(aff_bits, cap, ne):
    rows = aff_bits.shape[0]
    chunk = min(512, rows)
    return pl.pallas_call(
        functools.partial(_thresh_kernel, cap=cap, ne=ne, chunk=chunk),
        out_shape=[jax.ShapeDtypeStruct((8, LANES), jnp.int32)] * 2,
        compiler_params=pltpu.CompilerParams(vmem_limit_bytes=VMEM_LIMIT_BYTES),
        name="thresh",
    )(aff_bits)


def _plan_kernel(aff_ref, thr_ref, ntie_ref, ut_ref, slot_ref, cnt_ref, base_ref, used_ref, base_s, tie_s):
    @pl.when(pl.program_id(0) == 0)
    def _():
        base_s[...] = jnp.zeros_like(base_s)
        tie_s[...] = jnp.zeros_like(tie_s)

    a = aff_ref[...]
    thr = thr_ref[...]
    eq = a == thr
    eqf = jnp.where(eq, 1.0, 0.0)
    ut = ut_ref[...]
    tie_rank = tie_s[...] + jnp.dot(eqf.astype(bf16), ut, preferred_element_type=f32)
    sel = (a > thr) | (eq & (tie_rank < ntie_ref[...]))
    self_ = jnp.where(sel, 1.0, 0.0)
    rank = jnp.dot(self_.astype(bf16), ut, preferred_element_type=f32)
    cnt = jnp.sum(self_, axis=1, keepdims=True)
    base = base_s[...]
    slot_ref[...] = jnp.where(sel, base + rank, -1.0)
    cnt_ref[0] = cnt.astype(jnp.int32)
    base_ref[0] = base.astype(jnp.int32)
    new_base = base + jnp.ceil(cnt * (1.0 / BF16_SUBLANES)) * BF16_SUBLANES
    base_s[...] = new_base
    used_ref[...] = new_base.astype(jnp.int32)
    tie_s[...] = tie_s[...] + jnp.sum(eqf, axis=1, keepdims=True)


def _plan(aff_t, thr, ntie, ut):
    ne, n = aff_t.shape
    t = MOE_TILE
    tiles = n // t
    return pl.pallas_call(
        _plan_kernel,
        grid=(tiles,),
        in_specs=[pl.BlockSpec((ne, t), lambda i: (0, i)),
                  pl.BlockSpec((ne, 1), lambda i: (0, 0)),
                  pl.BlockSpec((ne, 1), lambda i: (0, 0)),
                  pl.BlockSpec((t, t), lambda i: (0, 0))],
        out_specs=[pl.BlockSpec((ne, t), lambda i: (0, i)),
                   pl.BlockSpec((1, ne, 1), lambda i: (i, 0, 0)),
                   pl.BlockSpec((1, ne, 1), lambda i: (i, 0, 0)),
                   pl.BlockSpec((ne, 1), lambda i: (0, 0))],
        out_shape=[jax.ShapeDtypeStruct((ne, n), f32),
                   jax.ShapeDtypeStruct((tiles, ne, 1), jnp.int32),
                   jax.ShapeDtypeStruct((tiles, ne, 1), jnp.int32),
                   jax.ShapeDtypeStruct((ne, 1), jnp.int32)],
        scratch_shapes=[pltpu.VMEM((ne, 1), f32), pltpu.VMEM((ne, 1), f32)],
        compiler_params=_params("arbitrary"),
        name="plan",
    )(aff_t, thr, ntie, ut)


def _blocks_needed(cnt_sm, tile, ne):
    mx = cnt_sm[tile * ne]
    for e in range(1, ne):
        mx = jnp.maximum(mx, cnt_sm[tile * ne + e])
    return (mx + SLOT_BLOCK - 1) // SLOT_BLOCK


def _gather_copy(stage, xe_hbm, sem, buf, e, start):
    r = SLOT_BLOCK
    return pltpu.make_async_copy(stage.at[buf, pl.ds(e * r, r), :], xe_hbm.at[e, pl.ds(start, r), :], sem.at[buf])


def _gather_kernel(base_sm, cnt_sm, used_sm, slot_ref, h_ref, xe_hbm, sel_s, stage, zeros_s, sem, state,
                   *, ne, group):
    i = pl.program_id(0)
    r = SLOT_BLOCK
    t = slot_ref.shape[1]
    rows_alloc = xe_hbm.shape[1]

    @pl.when(i == 0)
    def _():
        state[0] = 0
        state[1] = 0

    def wait_buf(buf):
        for e in range(ne):
            _gather_copy(stage, xe_hbm, sem, buf, e, 0).wait()

    def block(b, _):
        buf = state[0]
        row_iota = lax.broadcasted_iota(jnp.int32, (r, t), 0).astype(f32)
        for e in range(ne):
            start = (base_sm[i * ne + e] + b * r).astype(f32)
            sel_s[pl.ds(e * r, r), :] = jnp.where(slot_ref[e:e + 1, :] == row_iota + start, 1.0, 0.0).astype(bf16)
        h = h_ref[...]
        for c in range(ne // group):
            rows = pl.ds(c * group * r, group * r)
            stage[buf, rows, :] = jnp.dot(sel_s[rows, :], h, preferred_element_type=f32).astype(bf16)

        @pl.when(state[1] == 1)
        def _():
            wait_buf(1 - buf)

        for e in range(ne):
            start = pl.multiple_of(base_sm[i * ne + e] + b * r, BF16_SUBLANES)
            _gather_copy(stage, xe_hbm, sem, buf, e, start).start()
        state[0] = 1 - buf
        state[1] = 1
        return 0

    lax.fori_loop(0, _blocks_needed(cnt_sm, i, ne), block, 0)

    @pl.when((i == pl.num_programs(0) - 1) & (state[1] == 1))
    def _():
        wait_buf(1 - state[0])

    @pl.when(i == pl.num_programs(0) - 1)
    def _():
        zeros_s[...] = jnp.zeros_like(zeros_s)

        def fill_copy(e, start, rows):
            return pltpu.make_async_copy(zeros_s.at[pl.ds(0, rows), :], xe_hbm.at[e, pl.ds(start, rows), :],
                                         sem.at[2])

        small = BF16_SUBLANES
        for e in range(ne):
            used = used_sm[e]
            n_big = (rows_alloc - used) // r
            n_small = (rows_alloc - used - n_big * r) // small

            def fill_big(k, _, e=e, used=used):
                fill_copy(e, pl.multiple_of(used + k * r, small), r).start()
                return 0

            def fill_small(k, _, e=e, used=used, n_big=n_big):
                fill_copy(e, pl.multiple_of(used + n_big * r + k * small, small), small).start()
                return 0

            def drain_big(k, _, e=e):
                fill_copy(e, 0, r).wait()
                return 0

            def drain_small(k, _, e=e):
                fill_copy(e, 0, small).wait()
                return 0

            lax.fori_loop(0, n_big, fill_big, 0)
            lax.fori_loop(0, n_small, fill_small, 0)
            lax.fori_loop(0, n_big, drain_big, 0)
            lax.fori_loop(0, n_small, drain_small, 0)


def _gather(base, cnt, used, slot_t, h2, rows_alloc):
    ne, n = slot_t.shape
    d = h2.shape[1]
    t = MOE_TILE
    group = 4
    return pl.pallas_call(
        functools.partial(_gather_kernel, ne=ne, group=group),
        grid_spec=pltpu.PrefetchScalarGridSpec(
            num_scalar_prefetch=3,
            grid=(n // t,),
            in_specs=[pl.BlockSpec((ne, t), lambda i, *_: (0, i)),
                      pl.BlockSpec((t, d), lambda i, *_: (i, 0))],
            out_specs=pl.BlockSpec(memory_space=pl.ANY),
            scratch_shapes=[pltpu.VMEM((ne * SLOT_BLOCK, t), bf16),
                            pltpu.VMEM((2, ne * SLOT_BLOCK, d), bf16),
                            pltpu.VMEM((SLOT_BLOCK, d), bf16),
                            pltpu.SemaphoreType.DMA((3,)),
                            pltpu.SMEM((2,), jnp.int32)]),
        out_shape=jax.ShapeDtypeStruct((ne, rows_alloc, d), bf16),
        compiler_params=_params("arbitrary"),
        name="gather",
    )(base, cnt, used, slot_t, h2)


def _ffn_kernel(used_sm, x_ref, wg_ref, wu_ref, wd_ref, o_ref, wg_s, wu_s, wd_s):
    e = pl.program_id(0)
    tm = x_ref.shape[1]
    r0 = pl.program_id(1) * tm
    used = used_sm[e]

    @pl.when(pl.program_id(1) == 0)
    def _():
        wg_s[...] = wg_ref[0].astype(bf16)
        wu_s[...] = wu_ref[0].astype(bf16)
        wd_s[...] = wd_ref[0].astype(bf16)

    @pl.when(r0 < used)
    def _():
        x = x_ref[0]
        a = jnp.dot(x, wg_s[...], preferred_element_type=f32)
        b = jnp.dot(x, wu_s[...], preferred_element_type=f32)
        hid = (_silu(a) * b).astype(bf16)
        o_ref[0] = jnp.dot(hid, wd_s[...], preferred_element_type=f32).astype(bf16)

    @pl.when(r0 >= used)
    def _():
        o_ref[...] = jnp.zeros_like(o_ref)


def _ffn(used, xe, wg, wu, wd):
    ne, rows, d = xe.shape
    f = wg.shape[2]
    tm = FFN_TILE
    return pl.pallas_call(
        _ffn_kernel,
        grid_spec=pltpu.PrefetchScalarGridSpec(
            num_scalar_prefetch=1,
            grid=(ne, rows // tm),
            in_specs=[pl.BlockSpec((1, tm, d), lambda e, i, *_: (e, i, 0)),
                      pl.BlockSpec((1, d, f), lambda e, i, *_: (e, 0, 0)),
                      pl.BlockSpec((1, d, f), lambda e, i, *_: (e, 0, 0)),
                      pl.BlockSpec((1, f, d), lambda e, i, *_: (e, 0, 0))],
            out_specs=pl.BlockSpec((1, tm, d), lambda e, i, *_: (e, i, 0)),
            scratch_shapes=[pltpu.VMEM((d, f), bf16), pltpu.VMEM((d, f), bf16), pltpu.VMEM((f, d), bf16)]),
        out_shape=jax.ShapeDtypeStruct((ne, rows, d), bf16),
        compiler_params=_params("arbitrary", "arbitrary"),
        name="ffn",
    )(used, xe, wg, wu, wd)


def _combine_copy(ye_hbm, ybuf, sem, buf, e, start):
    r = SLOT_BLOCK
    return pltpu.make_async_copy(ye_hbm.at[e, pl.ds(start, r), :], ybuf.at[buf, pl.ds(e * SLOT_LANES, r), :],
                                 sem.at[buf])


def _combine_kernel(base_sm, cnt_sm, slot_ref, aff_ref, slot_nx, aff_nx, x1_ref, g2_ref, ye_hbm, o_ref,
                    sel_s, ybuf, sem, *, ne):
    j = pl.program_id(0)
    last = pl.num_programs(0) - 1
    r = SLOT_BLOCK
    t = MOE_TILE
    n_tiles = 2 * pl.num_programs(0)

    def start_block(tile, b, buf):
        for e in range(ne):
            start = pl.multiple_of(base_sm[tile * ne + e] + b * r, BF16_SUBLANES)
            _combine_copy(ye_hbm, ybuf, sem, buf, e, start).start()

    def wait_block(buf):
        for e in range(ne):
            _combine_copy(ye_hbm, ybuf, sem, buf, e, 0).wait()

    def one_hot(tile, b, sref, aref, row0, buf):
        lane = lax.broadcasted_iota(jnp.int32, (t, SLOT_LANES), 1)
        lane_iota = jnp.where(lane < r, lane, -(1 << 24)).astype(f32)
        rows = pl.ds(row0, t)
        for e in range(ne):
            start = (base_sm[tile * ne + e] + b * r).astype(f32)
            hit = sref[rows, e:e + 1] == lane_iota + start
            sel_s[buf, :, pl.ds(e * SLOT_LANES, SLOT_LANES)] = jnp.where(hit, aref[rows, e:e + 1], 0.0).astype(bf16)

    def finish(row0, buf):
        rows = pl.ds(row0, t)
        o_ref[rows, :] = x1_ref[rows, :] + g2_ref[0] * jnp.dot(sel_s[buf], ybuf[buf], preferred_element_type=f32)

    def extra_blocks(tile, row0):
        rows = pl.ds(row0, t)

        def extra(b, _):
            start_block(tile, b, 2)
            one_hot(tile, b, slot_ref, aff_ref, row0, 2)
            wait_block(2)
            o_ref[rows, :] += g2_ref[0] * jnp.dot(sel_s[2], ybuf[2], preferred_element_type=f32)
            return 0

        lax.fori_loop(1, jnp.maximum(_blocks_needed(cnt_sm, tile, ne), 1), extra, 0)

    t0 = 2 * j
    t1 = t0 + 1
    t2 = jnp.minimum(t0 + 2, n_tiles - 1)

    @pl.when(j == 0)
    def _():
        ybuf[...] = jnp.zeros(ybuf.shape, bf16)
        one_hot(0, 0, slot_ref, aff_ref, 0, 0)
        start_block(0, 0, 0)

    start_block(t1, 0, 1)
    wait_block(0)
    finish(0, 0)
    one_hot(t1, 0, slot_ref, aff_ref, t, 1)
    extra_blocks(t0, 0)

    @pl.when(j < last)
    def _():
        start_block(t2, 0, 0)

    wait_block(1)
    finish(t, 1)
    one_hot(t2, 0, slot_nx, aff_nx, 0, 0)
    extra_blocks(t1, t)


def _combine(base, cnt, slot, aff, x1, gate2, ye, seq):
    n, ne = slot.shape
    d = x1.shape[1]
    t = MOE_TILE
    tiles = n // t
    per_seq = seq // (2 * t)
    nxt = lambda j, *_: (jnp.minimum(2 * j + 2, tiles - 1), 0)
    return pl.pallas_call(
        functools.partial(_combine_kernel, ne=ne),
        grid_spec=pltpu.PrefetchScalarGridSpec(
            num_scalar_prefetch=2,
            grid=(tiles // 2,),
            in_specs=[pl.BlockSpec((2 * t, ne), lambda j, *_: (j, 0)),
                      pl.BlockSpec((2 * t, ne), lambda j, *_: (j, 0)),
                      pl.BlockSpec((t, ne), nxt),
                      pl.BlockSpec((t, ne), nxt),
                      pl.BlockSpec((2 * t, d), lambda j, *_: (j, 0)),
                      pl.BlockSpec((1, 1, d), lambda j, *_: (j // per_seq, 0, 0)),
                      pl.BlockSpec(memory_space=pl.ANY)],
            out_specs=pl.BlockSpec((2 * t, d), lambda j, *_: (j, 0)),
            scratch_shapes=[pltpu.VMEM((3, t, ne * SLOT_LANES), bf16),
                            pltpu.VMEM((3, ne * SLOT_LANES, d), bf16),
                            pltpu.SemaphoreType.DMA((3,))]),
        out_shape=jax.ShapeDtypeStruct((n, d), f32),
        compiler_params=_params("arbitrary"),
        name="combine",
    )(base, cnt, slot, aff, slot, aff, x1, gate2, ye)


def _rope_tables(s):
    t = jnp.arange(s)
    row = (t // GRID_W).astype(f32)
    col = (t % GRID_W).astype(f32)
    axis_dim = HEAD_DIM // 2
    inv = ROPE_THETA ** (-jnp.arange(0, axis_dim, 2, dtype=f32) / axis_dim)
    ang = jnp.concatenate([row[:, None] * inv, col[:, None] * inv], axis=-1)
    cos, sin = jnp.cos(ang), jnp.sin(ang)
    reps = LANES // HEAD_DIM
    cos_t = jnp.tile(jnp.concatenate([cos, cos], axis=-1), (1, reps))
    sin_t = jnp.tile(jnp.concatenate([-sin, sin], axis=-1), (1, reps))
    return cos_t, sin_t


def _head_perm(n_heads):
    half = jnp.concatenate([jnp.arange(0, HEAD_DIM, 2), jnp.arange(1, HEAD_DIM, 2)])
    return (jnp.arange(n_heads)[:, None] * HEAD_DIM + half[None, :]).reshape(-1)


def _block_mean(width):
    idx = jnp.arange(width) // HEAD_DIM
    return jnp.where(idx[:, None] == idx[None, :], 1.0 / HEAD_DIM, 0.0).astype(bf16)


def _prepare(norm_mix, w_in, q_norm, k_norm, conv_w, conv_b, conv_ln_g, conv_ln_b, w_out, norm_ffn,
             w_router, w_gate, w_up, w_down):
    d = w_in.shape[0]
    d_conv = conv_w.shape[-1]
    d_attn = w_out.shape[0] - d_conv
    d_kv = (w_in.shape[1] - d_attn - 2 * d_conv) // 2
    n_heads = d_attn // HEAD_DIM
    n_kv = d_kv // HEAD_DIM
    qp = _head_perm(n_heads)
    kp = _head_perm(n_kv)
    cols = jnp.concatenate([qp, d_attn + kp, jnp.arange(d_attn + d_kv, w_in.shape[1])])
    wr_hi = w_router.astype(bf16)
    wr_lo = (w_router - wr_hi.astype(f32)).astype(bf16)
    score_bound = HEAD_DIM * (HEAD_DIM ** -0.5 * LOG2E) * jnp.max(jnp.abs(q_norm)) * jnp.max(jnp.abs(k_norm))
    return dict(
        d_attn=d_attn, d_kv=d_kv, score_bound=score_bound,
        norm_mix=norm_mix.reshape(1, d),
        w_in=w_in[:, cols].astype(bf16),
        qg=jnp.tile(q_norm[_head_perm(1)], n_heads).reshape(1, d_attn),
        kg=jnp.tile(k_norm[_head_perm(1)], n_kv).reshape(1, d_kv),
        bdq=_block_mean(d_attn), bdk=_block_mean(d_kv),
        conv_w=conv_w.reshape(conv_w.shape[0], d_conv), conv_b=conv_b.reshape(1, d_conv),
        conv_ln_g=conv_ln_g.reshape(1, d_conv), conv_ln_b=conv_ln_b.reshape(1, d_conv),
        w_a=w_out[:d_attn].astype(bf16), w_c=w_out[d_attn:].astype(bf16),
        norm_ffn=norm_ffn.reshape(1, d),
        wr1=jnp.concatenate([wr_hi, wr_lo], axis=1), wr2=wr_hi,
        w_gate=w_gate, w_up=w_up, w_down=w_down,
    )


def _encoder_layer(x, mod, p):
    b, s, d = x.shape
    shift1, scale1, gate1, shift2, scale2, gate2 = [m.reshape(b, 1, d) for m in jnp.split(mod, 6, axis=-1)]
    cos, sin = _rope_tables(s)
    q, k, v, u = _inproj(x, shift1, scale1, p["norm_mix"], p["w_in"], p["qg"], p["kg"], p["bdq"], p["bdk"],
                         cos, sin, p["d_attn"], p["d_kv"])
    attn = lax.cond(p["score_bound"] <= SCORE_BOUND_LOG2,
                    lambda: _attention(q, k, v, True), lambda: _attention(q, k, v, False))
    conv = _conv(u, p["conv_w"], p["conv_b"], p["conv_ln_g"], p["conv_ln_b"])
    x1, h2, aff = _outproj(attn, conv, x, gate1, scale2, shift2, p["norm_ffn"], p["w_a"], p["w_c"],
                           p["wr1"], p["wr2"])

    n = b * s
    ne = aff.shape[-1]
    cap = CAPACITY_FACTOR * n // ne
    aff = aff.reshape(n, ne)
    bits = lax.bitcast_convert_type(aff, jnp.int32).reshape(n * ne // LANES, LANES)
    thr_bits, ntie = _thresh(bits, cap, ne)
    thr = lax.bitcast_convert_type(thr_bits[0, :ne], f32).reshape(ne, 1)
    ntie = ntie[0, :ne].astype(f32).reshape(ne, 1)
    t = MOE_TILE
    tok = jnp.arange(t)
    ut = (tok[:, None] < tok[None, :]).astype(bf16)
    slot_t, cnt, base, used = _plan(aff.T, thr, ntie, ut)
    cnt = cnt.reshape(-1)
    base = base.reshape(-1)
    tiles = n // t
    rows_alloc = -(-(cap + BF16_SUBLANES * tiles + SLOT_BLOCK) // FFN_TILE) * FFN_TILE
    used = used.reshape(-1)
    xe = _gather(base, cnt, used, slot_t, h2.reshape(n, d), rows_alloc)
    ye = _ffn(used, xe, p["w_gate"], p["w_up"], p["w_down"])
    out = _combine(base, cnt, slot_t.T, aff, x1.reshape(n, d), gate2, ye, s)
    return out.reshape(b, s, d)


def kernel(x_prompt, x_sample, c_prompt, c_sample, ada_w, ada_b, norm_mix, w_in, q_norm, k_norm, conv_w, conv_b,
           conv_ln_g, conv_ln_b, w_out, norm_ffn, w_router, w_gate, w_up, w_down):
    y_prompt, y_sample = x_prompt, x_sample
    nb = x_prompt.shape[0]
    for l in range(ada_w.shape[0]):
        p = _prepare(norm_mix[l], w_in[l], q_norm[l], k_norm[l], conv_w[l], conv_b[l], conv_ln_g[l],
                     conv_ln_b[l], w_out[l], norm_ffn[l], w_router[l], w_gate[l], w_up[l], w_down[l])
        mod = _ada(jnp.concatenate([c_prompt, c_sample], axis=0), ada_w[l], ada_b[l])
        y_prompt = _encoder_layer(y_prompt, mod[:nb], p)
        y_sample = _encoder_layer(y_sample, mod[nb:], p)
    return (y_prompt, y_sample)
```

```python
import functools

import jax
import jax.numpy as jnp
from jax import lax
from jax.experimental import pallas as pl
from jax.experimental.pallas import tpu as pltpu

HEAD_DIM = 64
N_KV_HEADS = 2
GRID_W = 64
ROPE_THETA = 10000.0
CAPACITY_FACTOR = 2
EPS = 1e-6
LOG2E = 1.4426950408889634

LANES = 128
BF16_SUBLANES = 16
VMEM_LIMIT_BYTES = 48 * 1024 * 1024

TOKEN_TILE = 1024
MOE_TILE = 512
SLOT_BLOCK = 128
SLOT_LANES = LANES
FFN_TILE = 512
Q_TILE = 256
K_TILE = 256
SHORT_SEQ_BLOCKS = 8
ATTN_STRIP = 256
SCORE_BOUND_LOG2 = 56.0
ATTN_BLOCKS_PER_TRIP = 16
CONV_TILE = 512
CONV_HALO = 16
CONV_ROWS = 128

f32 = jnp.float32
bf16 = jnp.bfloat16


def _params(*sem):
    return pltpu.CompilerParams(dimension_semantics=sem, vmem_limit_bytes=VMEM_LIMIT_BYTES)


def _silu(x):
    return x * jax.nn.sigmoid(x)


def _ada_kernel(c_ref, w_ref, b_ref, o_ref):
    c = c_ref[...]
    o_ref[...] = jnp.dot(_silu(c), w_ref[...], preferred_element_type=f32,
                         precision=lax.Precision.HIGHEST) + b_ref[...]


def _ada(c, w, b):
    r, d = c.shape
    n = w.shape[1]
    tn = n // 4
    return pl.pallas_call(
        _ada_kernel,
        grid=(n // tn,),
        in_specs=[pl.BlockSpec((r, d), lambda j: (0, 0)),
                  pl.BlockSpec((d, tn), lambda j: (0, j)),
                  pl.BlockSpec((1, tn), lambda j: (0, j))],
        out_specs=pl.BlockSpec((r, tn), lambda j: (0, j)),
        out_shape=jax.ShapeDtypeStruct((r, n), f32),
        compiler_params=_params("arbitrary"),
        name="ada",
    )(c, w, b.reshape(1, n))


def _norm_rope(t, bd_ref, gain_ref, cos, sin, scale):
    ms = jnp.dot((t * t).astype(bf16), bd_ref[...], preferred_element_type=f32)
    tn = t * lax.rsqrt(ms + EPS) * gain_ref[...]
    w = t.shape[1]
    reps = w // LANES
    if reps > 1:
        cos = jnp.concatenate([cos] * reps, axis=1)
        sin = jnp.concatenate([sin] * reps, axis=1)
    lane = lax.broadcasted_iota(jnp.int32, tn.shape, 1)
    first_half = (lane % HEAD_DIM) < (HEAD_DIM // 2)
    partner = jnp.where(first_half, pltpu.roll(tn, w - HEAD_DIM // 2, 1), pltpu.roll(tn, HEAD_DIM // 2, 1))
    return (tn * cos + partner * sin) * scale


def _inproj_kernel(x_ref, sh_ref, sc_ref, g_ref, w_ref, qg_ref, kg_ref, bdq_ref, bdk_ref, cos_ref, sin_ref,
                   q_ref, k_ref, v_ref, u_ref, *, d_attn, d_kv):
    x = x_ref[0]
    ms = jnp.mean(x * x, axis=-1, keepdims=True)
    h = x * lax.rsqrt(ms + EPS) * g_ref[...]
    h = h * (1.0 + sc_ref[0]) + sh_ref[0]
    z = jnp.dot(h.astype(bf16), w_ref[...], preferred_element_type=f32)
    cos = cos_ref[...]
    sin = sin_ref[...]
    q = _norm_rope(z[:, :d_attn], bdq_ref, qg_ref, cos, sin, HEAD_DIM ** -0.5 * LOG2E)
    k = _norm_rope(z[:, d_attn:d_attn + d_kv], bdk_ref, kg_ref, cos, sin, 1.0)
    v = z[:, d_attn + d_kv:d_attn + 2 * d_kv]
    q_ref[0] = q.T.astype(bf16)
    vt = v.T.astype(bf16)
    for g in range(d_kv // HEAD_DIM):
        k_ref[0, g] = k[:, g * HEAD_DIM:(g + 1) * HEAD_DIM].astype(bf16)
        for j in range(v_ref.shape[2]):
            v_ref[0, g, j] = vt[g * HEAD_DIM:(g + 1) * HEAD_DIM, j * K_TILE:(j + 1) * K_TILE]
    u_ref[0] = z[:, d_attn + 2 * d_kv:].astype(bf16)


def _inproj(x, shift, scale, gain, w_in, qg, kg, bdq, bdk, cos, sin, d_attn, d_kv):
    b, s, d = x.shape
    d_in = w_in.shape[1]
    d_u = d_in - d_attn - 2 * d_kv
    n_kv = d_kv // HEAD_DIM
    tm = TOKEN_TILE
    const = lambda shape: pl.BlockSpec(shape, lambda bi, i: (0,) * len(shape))
    per_b = pl.BlockSpec((1, 1, d), lambda bi, i: (bi, 0, 0))
    return pl.pallas_call(
        functools.partial(_inproj_kernel, d_attn=d_attn, d_kv=d_kv),
        grid=(b, s // tm),
        in_specs=[pl.BlockSpec((1, tm, d), lambda bi, i: (bi, i, 0)), per_b, per_b,
                  const((1, d)), const((d, d_in)), const((1, d_attn)), const((1, d_kv)),
                  const((d_attn, d_attn)), const((d_kv, d_kv)),
                  pl.BlockSpec((tm, LANES), lambda bi, i: (i, 0)),
                  pl.BlockSpec((tm, LANES), lambda bi, i: (i, 0))],
        out_specs=[pl.BlockSpec((1, d_attn, tm), lambda bi, i: (bi, 0, i)),
                   pl.BlockSpec((1, n_kv, tm, HEAD_DIM), lambda bi, i: (bi, 0, i, 0)),
                   pl.BlockSpec((1, n_kv, tm // K_TILE, HEAD_DIM, K_TILE), lambda bi, i: (bi, 0, i, 0, 0)),
                   pl.BlockSpec((1, tm, d_u), lambda bi, i: (bi, i, 0))],
        out_shape=[jax.ShapeDtypeStruct((b, d_attn, s), bf16),
                   jax.ShapeDtypeStruct((b, n_kv, s, HEAD_DIM), bf16),
                   jax.ShapeDtypeStruct((b, n_kv, s // K_TILE, HEAD_DIM, K_TILE), bf16),
                   jax.ShapeDtypeStruct((b, s, d_u), bf16)],
        compiler_params=_params("arbitrary", "arbitrary"),
        name="inproj",
    )(x, shift, scale, gain, w_in, qg, kg, bdq, bdk, cos, sin)


def _attn_kernel(q_ref, k_ref, v_ref, o_ref, s_scr, q_scr, m_scr, acc_scr, *, n_rep, bounded):
    tq = q_ref.shape[2]
    n_blk, _, tk = v_ref.shape[2:]
    qt = q_ref[0]
    qs = jnp.concatenate([qt[h * HEAD_DIM:(h + 1) * HEAD_DIM, :] for h in range(n_rep)], axis=1)
    cols = qs.shape[1]
    ones = jnp.ones((BF16_SUBLANES, tk), bf16)

    strips = [pl.ds(c, ATTN_STRIP) for c in range(0, cols, ATTN_STRIP)]

    def scores(i, slot):
        kt = k_ref[0, 0, pl.ds(pl.multiple_of(jnp.minimum(i, n_blk - 1) * tk, tk), tk), :]
        for c in strips:
            s_scr[slot, :, c] = jnp.dot(kt, q_scr[:, c], preferred_element_type=f32)

    def attend(i, slot):
        vt = jnp.concatenate([v_ref[0, 0, i], ones], axis=0)
        for c in strips:
            if bounded:
                p = jnp.exp2(s_scr[slot, :, c]).astype(bf16)
                acc_scr[:, c] += jnp.dot(vt, p, preferred_element_type=f32)
            else:
                m = m_scr[:, c]
                m_new = jnp.maximum(m, jnp.max(s_scr[slot, :, c], axis=0, keepdims=True))
                p = jnp.exp2(s_scr[slot, :, c] - m_new).astype(bf16)
                acc_scr[:, c] = jnp.exp2(m - m_new) * acc_scr[:, c] + jnp.dot(vt, p, preferred_element_type=f32)
                m_scr[:, c] = m_new

    per_trip = min(ATTN_BLOCKS_PER_TRIP, n_blk)

    def body(j, _):
        for step in range(per_trip):
            i = j * per_trip + step
            slot = step % 2
            scores(i + 1, 1 - slot)
            attend(i, slot)
        return 0

    q_scr[...] = qs
    m_scr[...] = jnp.full(m_scr.shape, -jnp.inf, f32)
    acc_scr[...] = jnp.zeros(acc_scr.shape, f32)
    scores(0, 0)
    lax.fori_loop(0, n_blk // per_trip, body, 0)
    acc = acc_scr[...]
    o = acc[:HEAD_DIM] / acc[HEAD_DIM:HEAD_DIM + 1]
    o_ref[0] = jnp.concatenate([o[:, h * tq:(h + 1) * tq] for h in range(n_rep)], axis=0).astype(bf16)


def _attention(qt, k, vt, bounded):
    b, d_attn, s = qt.shape
    n_kv, n_blk, _, tk = vt.shape[1:]
    n_rep = d_attn // HEAD_DIM // n_kv
    tq = min(s, Q_TILE * (2 if n_blk <= SHORT_SEQ_BLOCKS else 1))
    gw = n_rep * HEAD_DIM
    return pl.pallas_call(
        functools.partial(_attn_kernel, n_rep=n_rep, bounded=bounded),
        grid=(b, n_kv, s // tq),
        in_specs=[pl.BlockSpec((1, gw, tq), lambda bi, g, i: (bi, g, i)),
                  pl.BlockSpec((1, 1, s, HEAD_DIM), lambda bi, g, i: (bi, g, 0, 0)),
                  pl.BlockSpec((1, 1, n_blk, HEAD_DIM, tk), lambda bi, g, i: (bi, g, 0, 0, 0))],
        out_specs=pl.BlockSpec((1, gw, tq), lambda bi, g, i: (bi, g, i)),
        out_shape=jax.ShapeDtypeStruct((b, d_attn, s), bf16),
        scratch_shapes=[pltpu.VMEM((2, tk, n_rep * tq), f32),
                        pltpu.VMEM((HEAD_DIM, n_rep * tq), bf16), pltpu.VMEM((1, n_rep * tq), f32),
                        pltpu.VMEM((HEAD_DIM + BF16_SUBLANES, n_rep * tq), f32)],
        compiler_params=_params("arbitrary", "arbitrary", "arbitrary"),
        name="attn_bounded" if bounded else "attn",
    )(qt, k, vt)


def _conv_kernel(u_ref, up_ref, un_ref, w_ref, cb_ref, lg_ref, lb_ref, o_ref, a_ref, y_ref, *, ts, dc, width):
    i = pl.program_id(1)
    last = pl.num_programs(1) - 1
    halo = CONV_HALO

    def glu(u):
        u = u.astype(f32)
        return u[:, :dc] * jax.nn.sigmoid(u[:, dc:])

    a_ref[pl.ds(halo, ts), :] = glu(u_ref[0])
    a_ref[pl.ds(0, halo), :] = jnp.where(i > 0, glu(up_ref[0]), 0.0)
    a_ref[pl.ds(halo + ts, halo), :] = jnp.where(i < last, glu(un_ref[0]), 0.0)

    off = halo - width // 2
    sub = 8
    rows = CONV_ROWS
    reach = -(-(off + width) // sub) * sub
    cb = cb_ref[...]
    lg = lg_ref[...]
    lb = lb_ref[...]
    for c in range(ts // rows):
        r0 = c * rows
        for g in range(dc // LANES):
            lanes = pl.ds(g * LANES, LANES)
            blk = a_ref[pl.ds(r0, rows + reach), lanes]
            acc = None
            for r in range(sub):
                part = None
                for u in range(r, off + width, sub):
                    if u < off:
                        continue
                    term = blk[u - r:u - r + rows + sub, :] * w_ref[pl.ds(u - off, 1), lanes]
                    part = term if part is None else part + term
                shifted = part[r:r + rows, :]
                acc = shifted if acc is None else acc + shifted
            y_ref[pl.ds(r0, rows), lanes] = acc
        y = y_ref[pl.ds(r0, rows), :] + cb
        mu = jnp.mean(y, axis=-1, keepdims=True)
        yc = y - mu
        var = jnp.mean(yc * yc, axis=-1, keepdims=True)
        yn = yc * lax.rsqrt(var + EPS) * lg + lb
        o_ref[0, pl.ds(r0, CONV_ROWS), :] = _silu(yn).astype(bf16)


def _conv(u, w, cb, lg, lb):
    b, s, du = u.shape
    dc = du // 2
    width = w.shape[0]
    ts = min(CONV_TILE, s)
    hb = ts // CONV_HALO
    n_h = s // CONV_HALO
    const = lambda shape: pl.BlockSpec(shape, lambda bi, i: (0,) * len(shape))
    return pl.pallas_call(
        functools.partial(_conv_kernel, ts=ts, dc=dc, width=width),
        grid=(b, s // ts),
        in_specs=[pl.BlockSpec((1, ts, du), lambda bi, i: (bi, i, 0)),
                  pl.BlockSpec((1, CONV_HALO, du), lambda bi, i: (bi, jnp.maximum(i * hb - 1, 0), 0)),
                  pl.BlockSpec((1, CONV_HALO, du), lambda bi, i: (bi, jnp.minimum((i + 1) * hb, n_h - 1), 0)),
                  const((width, dc)), const((1, dc)), const((1, dc)), const((1, dc))],
        out_specs=pl.BlockSpec((1, ts, dc), lambda bi, i: (bi, i, 0)),
        out_shape=jax.ShapeDtypeStruct((b, s, dc), bf16),
        scratch_shapes=[pltpu.VMEM((ts + 2 * CONV_HALO, dc), f32), pltpu.VMEM((ts, dc), f32)],
        compiler_params=_params("arbitrary", "arbitrary"),
        name="conv",
    )(u, u, u, w, cb, lg, lb)


def _outproj_kernel(a_ref, c_ref, x_ref, g1_ref, sc_ref, sh_ref, nf_ref, wa_ref, wc_ref, wr1_ref, wr2_ref,
                    x1_ref, h2_ref, aff_ref, *, ne):
    attn = a_ref[0].astype(f32).T.astype(bf16)
    mix = (jnp.dot(attn, wa_ref[...], preferred_element_type=f32)
           + jnp.dot(c_ref[0], wc_ref[...], preferred_element_type=f32))
    x1 = x_ref[0] + g1_ref[0] * mix
    x1_ref[0] = x1
    ms = jnp.mean(x1 * x1, axis=-1, keepdims=True)
    h = x1 * lax.rsqrt(ms + EPS) * nf_ref[...]
    h2 = h * (1.0 + sc_ref[0]) + sh_ref[0]
    hi = h2.astype(bf16)
    lo = (h2 - hi.astype(f32)).astype(bf16)
    h2_ref[0] = hi
    r1 = jnp.dot(hi, wr1_ref[...], preferred_element_type=f32)
    r2 = jnp.dot(lo, wr2_ref[...], preferred_element_type=f32)
    logits = r1[:, :ne] + r1[:, ne:] + r2
    m = jnp.max(logits, axis=-1, keepdims=True)
    e = jnp.exp(logits - m)
    aff_ref[0] = e / jnp.sum(e, axis=-1, keepdims=True)


def _outproj(attn, conv, x, gate1, scale2, shift2, norm_ffn, w_a, w_c, wr1, wr2):
    b, s, d = x.shape
    da = attn.shape[1]
    dc = conv.shape[2]
    ne = wr2.shape[1]
    tm = TOKEN_TILE
    const = lambda shape: pl.BlockSpec(shape, lambda bi, i: (0,) * len(shape))
    per_b = pl.BlockSpec((1, 1, d), lambda bi, i: (bi, 0, 0))
    return pl.pallas_call(
        functools.partial(_outproj_kernel, ne=ne),
        grid=(b, s // tm),
        in_specs=[pl.BlockSpec((1, da, tm), lambda bi, i: (bi, 0, i)),
                  pl.BlockSpec((1, tm, dc), lambda bi, i: (bi, i, 0)),
                  pl.BlockSpec((1, tm, d), lambda bi, i: (bi, i, 0)),
                  per_b, per_b, per_b, const((1, d)), const((da, d)), const((dc, d)),
                  const((d, 2 * ne)), const((d, ne))],
        out_specs=[pl.BlockSpec((1, tm, d), lambda bi, i: (bi, i, 0)),
                   pl.BlockSpec((1, tm, d), lambda bi, i: (bi, i, 0)),
                   pl.BlockSpec((1, tm, ne), lambda bi, i: (bi, i, 0))],
        out_shape=[jax.ShapeDtypeStruct((b, s, d), f32),
                   jax.ShapeDtypeStruct((b, s, d), bf16),
                   jax.ShapeDtypeStruct((b, s, ne), f32)],
        compiler_params=_params("arbitrary", "arbitrary"),
        name="outproj",
    )(attn, conv, x, gate1, scale2, shift2, norm_ffn, w_a, w_c, wr1, wr2)


def _thresh_kernel(bits_ref, thr_ref, ntie_ref, *, cap, ne, chunk):
    rows = bits_ref.shape[0]

    def count_ge(cand):
        cand_row = cand[0:1, :]

        def body(i, acc):
            blk = bits_ref[pl.ds(pl.multiple_of(i * chunk, chunk), chunk), :]
            hit = jnp.where(blk >= cand_row, 1, 0).astype(jnp.int32)
            return acc + jnp.sum(hit.reshape(chunk // 8, 8, LANES), axis=0)
        acc = lax.fori_loop(0, rows // chunk, body, jnp.zeros((8, LANES), jnp.int32))
        tot = jnp.sum(acc, axis=0, keepdims=True)
        tot = jnp.broadcast_to(tot, (8, LANES))
        shift = LANES // 2
        while shift >= ne:
            tot = tot + pltpu.roll(tot, shift, 1)
            shift //= 2
        return tot

    def bit_step(i, prefix):
        cand = prefix | jnp.left_shift(jnp.int32(1), 30 - i)
        return jnp.where(count_ge(cand) >= cap, cand, prefix)

    prefix = lax.fori_loop(0, 31, bit_step, jnp.zeros((8, LANES), jnp.int32))
    thr_ref[...] = prefix
    ntie_ref[...] = cap - count_ge(prefix + 1)


def _thresh(aff_bits, cap, ne):
    rows = aff_bits.shape[0]
    chunk = min(512, rows)
    return pl.pallas_call(
        functools.partial(_thresh_kernel, cap=cap, ne=ne, chunk=chunk),
        out_shape=[jax.ShapeDtypeStruct((8, LANES), jnp.int32)] * 2,
        compiler_params=pltpu.CompilerParams(vmem_limit_bytes=VMEM_LIMIT_BYTES),
        name="thresh",
    )(aff_bits)


def _plan_kernel(aff_ref, thr_ref, ntie_ref, ut_ref, slot_ref, cnt_ref, base_ref, used_ref, base_s, tie_s):
    @pl.when(pl.program_id(0) == 0)
    def _():
        base_s[...] = jnp.zeros_like(base_s)
        tie_s[...] = jnp.zeros_like(tie_s)

    a = aff_ref[...]
    thr = thr_ref[...]
    eq = a == thr
    eqf = jnp.where(eq, 1.0, 0.0)
    ut = ut_ref[...]
    tie_rank = tie_s[...] + jnp.dot(eqf.astype(bf16), ut, preferred_element_type=f32)
    sel = (a > thr) | (eq & (tie_rank < ntie_ref[...]))
    self_ = jnp.where(sel, 1.0, 0.0)
    rank = jnp.dot(self_.astype(bf16), ut, preferred_element_type=f32)
    cnt = jnp.sum(self_, axis=1, keepdims=True)
    base = base_s[...]
    slot_ref[...] = jnp.where(sel, base + rank, -1.0)
    cnt_ref[0] = cnt.astype(jnp.int32)
    base_ref[0] = base.astype(jnp.int32)
    new_base = base + jnp.ceil(cnt * (1.0 / BF16_SUBLANES)) * BF16_SUBLANES
    base_s[...] = new_base
    used_ref[...] = new_base.astype(jnp.int32)
    tie_s[...] = tie_s[...] + jnp.sum(eqf, axis=1, keepdims=True)


def _plan(aff_t, thr, ntie, ut):
    ne, n = aff_t.shape
    t = MOE_TILE
    tiles = n // t
    return pl.pallas_call(
        _plan_kernel,
        grid=(tiles,),
        in_specs=[pl.BlockSpec((ne, t), lambda i: (0, i)),
                  pl.BlockSpec((ne, 1), lambda i: (0, 0)),
                  pl.BlockSpec((ne, 1), lambda i: (0, 0)),
                  pl.BlockSpec((t, t), lambda i: (0, 0))],
        out_specs=[pl.BlockSpec((ne, t), lambda i: (0, i)),
                   pl.BlockSpec((1, ne, 1), lambda i: (i, 0, 0)),
                   pl.BlockSpec((1, ne, 1), lambda i: (i, 0, 0)),
                   pl.BlockSpec((ne, 1), lambda i: (0, 0))],
        out_shape=[jax.ShapeDtypeStruct((ne, n), f32),
                   jax.ShapeDtypeStruct((tiles, ne, 1), jnp.int32),
                   jax.ShapeDtypeStruct((tiles, ne, 1), jnp.int32),
                   jax.ShapeDtypeStruct((ne, 1), jnp.int32)],
        scratch_shapes=[pltpu.VMEM((ne, 1), f32), pltpu.VMEM((ne, 1), f32)],
        compiler_params=_params("arbitrary"),
        name="plan",
    )(aff_t, thr, ntie, ut)


def _blocks_needed(cnt_sm, tile, ne):
    mx = cnt_sm[tile * ne]
    for e in range(1, ne):
        mx = jnp.maximum(mx, cnt_sm[tile * ne + e])
    return (mx + SLOT_BLOCK - 1) // SLOT_BLOCK


def _gather_copy(stage, xe_hbm, sem, buf, e, start):
    r = SLOT_BLOCK
    return pltpu.make_async_copy(stage.at[buf, pl.ds(e * r, r), :], xe_hbm.at[e, pl.ds(start, r), :], sem.at[buf])


def _gather_kernel(base_sm, cnt_sm, used_sm, slot_ref, h_ref, xe_hbm, sel_s, stage, zeros_s, sem, state,
                   *, ne, group):
    i = pl.program_id(0)
    r = SLOT_BLOCK
    t = slot_ref.shape[1]
    rows_alloc = xe_hbm.shape[1]

    @pl.when(i == 0)
    def _():
        state[0] = 0
        state[1] = 0

    def wait_buf(buf):
        for e in range(ne):
            _gather_copy(stage, xe_hbm, sem, buf, e, 0).wait()

    def block(b, _):
        buf = state[0]
        row_iota = lax.broadcasted_iota(jnp.int32, (r, t), 0).astype(f32)
        for e in range(ne):
            start = (base_sm[i * ne + e] + b * r).astype(f32)
            sel_s[pl.ds(e * r, r), :] = jnp.where(slot_ref[e:e + 1, :] == row_iota + start, 1.0, 0.0).astype(bf16)
        h = h_ref[...]
        for c in range(ne // group):
            rows = pl.ds(c * group * r, group * r)
            stage[buf, rows, :] = jnp.dot(sel_s[rows, :], h, preferred_element_type=f32).astype(bf16)

        @pl.when(state[1] == 1)
        def _():
            wait_buf(1 - buf)

        for e in range(ne):
            start = pl.multiple_of(base_sm[i * ne + e] + b * r, BF16_SUBLANES)
            _gather_copy(stage, xe_hbm, sem, buf, e, start).start()
        state[0] = 1 - buf
        state[1] = 1
        return 0

    lax.fori_loop(0, _blocks_needed(cnt_sm, i, ne), block, 0)

    @pl.when((i == pl.num_programs(0) - 1) & (state[1] == 1))
    def _():
        wait_buf(1 - state[0])

    @pl.when(i == pl.num_programs(0) - 1)
    def _():
        zeros_s[...] = jnp.zeros_like(zeros_s)

        def fill_copy(e, start, rows):
            return pltpu.make_async_copy(zeros_s.at[pl.ds(0, rows), :], xe_hbm.at[e, pl.ds(start, rows), :],
                                         sem.at[2])

        small = BF16_SUBLANES
        for e in range(ne):
            used = used_sm[e]
            n_big = (rows_alloc - used) // r
            n_small = (rows_alloc - used - n_big * r) // small

            def fill_big(k, _, e=e, used=used):
                fill_copy(e, pl.multiple_of(used + k * r, small), r).start()
                return 0

            def fill_small(k, _, e=e, used=used, n_big=n_big):
                fill_copy(e, pl.multiple_of(used + n_big * r + k * small, small), small).start()
                return 0

            def drain_big(k, _, e=e):
                fill_copy(e, 0, r).wait()
                return 0

            def drain_small(k, _, e=e):
                fill_copy(e, 0, small).wait()
                return 0

            lax.fori_loop(0, n_big, fill_big, 0)
            lax.fori_loop(0, n_small, fill_small, 0)
            lax.fori_loop(0, n_big, drain_big, 0)
            lax.fori_loop(0, n_small, drain_small, 0)


def _gather(base, cnt, used, slot_t, h2, rows_alloc):
    ne, n = slot_t.shape
    d = h2.shape[1]
    t = MOE_TILE
    group = 4
    return pl.pallas_call(
        functools.partial(_gather_kernel, ne=ne, group=group),
        grid_spec=pltpu.PrefetchScalarGridSpec(
            num_scalar_prefetch=3,
            grid=(n // t,),
            in_specs=[pl.BlockSpec((ne, t), lambda i, *_: (0, i)),
                      pl.BlockSpec((t, d), lambda i, *_: (i, 0))],
            out_specs=pl.BlockSpec(memory_space=pl.ANY),
            scratch_shapes=[pltpu.VMEM((ne * SLOT_BLOCK, t), bf16),
                            pltpu.VMEM((2, ne * SLOT_BLOCK, d), bf16),
                            pltpu.VMEM((SLOT_BLOCK, d), bf16),
                            pltpu.SemaphoreType.DMA((3,)),
                            pltpu.SMEM((2,), jnp.int32)]),
        out_shape=jax.ShapeDtypeStruct((ne, rows_alloc, d), bf16),
        compiler_params=_params("arbitrary"),
        name="gather",
    )(base, cnt, used, slot_t, h2)


def _ffn_kernel(used_sm, x_ref, wg_ref, wu_ref, wd_ref, o_ref, wg_s, wu_s, wd_s):
    e = pl.program_id(0)
    tm = x_ref.shape[1]
    r0 = pl.program_id(1) * tm
    used = used_sm[e]

    @pl.when(pl.program_id(1) == 0)
    def _():
        wg_s[...] = wg_ref[0].astype(bf16)
        wu_s[...] = wu_ref[0].astype(bf16)
        wd_s[...] = wd_ref[0].astype(bf16)

    @pl.when(r0 < used)
    def _():
        x = x_ref[0]
        a = jnp.dot(x, wg_s[...], preferred_element_type=f32)
        b = jnp.dot(x, wu_s[...], preferred_element_type=f32)
        hid = (_silu(a) * b).astype(bf16)
        o_ref[0] = jnp.dot(hid, wd_s[...], preferred_element_type=f32).astype(bf16)

    @pl.when(r0 >= used)
    def _():
        o_ref[...] = jnp.zeros_like(o_ref)


def _ffn(used, xe, wg, wu, wd):
    ne, rows, d = xe.shape
    f = wg.shape[2]
    tm = FFN_TILE
    return pl.pallas_call(
        _ffn_kernel,
        grid_spec=pltpu.PrefetchScalarGridSpec(
            num_scalar_prefetch=1,
            grid=(ne, rows // tm),
            in_specs=[pl.BlockSpec((1, tm, d), lambda e, i, *_: (e, i, 0)),
                      pl.BlockSpec((1, d, f), lambda e, i, *_: (e, 0, 0)),
                      pl.BlockSpec((1, d, f), lambda e, i, *_: (e, 0, 0)),
                      pl.BlockSpec((1, f, d), lambda e, i, *_: (e, 0, 0))],
            out_specs=pl.BlockSpec((1, tm, d), lambda e, i, *_: (e, i, 0)),
            scratch_shapes=[pltpu.VMEM((d, f), bf16), pltpu.VMEM((d, f), bf16), pltpu.VMEM((f, d), bf16)]),
        out_shape=jax.ShapeDtypeStruct((ne, rows, d), bf16),
        compiler_params=_params("arbitrary", "arbitrary"),
        name="ffn",
    )(used, xe, wg, wu, wd)


def _combine_copy(ye_hbm, ybuf, sem, buf, e, start):
    r = SLOT_BLOCK
    return pltpu.make_async_copy(ye_hbm.at[e, pl.ds(start, r), :], ybuf.at[buf, pl.ds(e * SLOT_LANES, r), :],
                                 sem.at[buf])


def _combine_kernel(base_sm, cnt_sm, slot_ref, aff_ref, slot_nx, aff_nx, x1_ref, g2_ref, ye_hbm, o_ref,
                    sel_s, ybuf, sem, *, ne):
    j = pl.program_id(0)
    last = pl.num_programs(0) - 1
    r = SLOT_BLOCK
    t = MOE_TILE
    n_tiles = 2 * pl.num_programs(0)

    def start_block(tile, b, buf):
        for e in range(ne):
            start = pl.multiple_of(base_sm[tile * ne + e] + b * r, BF16_SUBLANES)
            _combine_copy(ye_hbm, ybuf, sem, buf, e, start).start()

    def wait_block(buf):
        for e in range(ne):
            _combine_copy(ye_hbm, ybuf, sem, buf, e, 0).wait()

    def one_hot(tile, b, sref, aref, row0, buf):
        lane = lax.broadcasted_iota(jnp.int32, (t, SLOT_LANES), 1)
        lane_iota = jnp.where(lane < r, lane, -(1 << 24)).astype(f32)
        rows = pl.ds(row0, t)
        for e in range(ne):
            start = (base_sm[tile * ne + e] + b * r).astype(f32)
            hit = sref[rows, e:e + 1] == lane_iota + start
            sel_s[buf, :, pl.ds(e * SLOT_LANES, SLOT_LANES)] = jnp.where(hit, aref[rows, e:e + 1], 0.0).astype(bf16)

    def finish(row0, buf):
        rows = pl.ds(row0, t)
        o_ref[rows, :] = x1_ref[rows, :] + g2_ref[0] * jnp.dot(sel_s[buf], ybuf[buf], preferred_element_type=f32)

    def extra_blocks(tile, row0):
        rows = pl.ds(row0, t)

        def extra(b, _):
            start_block(tile, b, 2)
            one_hot(tile, b, slot_ref, aff_ref, row0, 2)
            wait_block(2)
            o_ref[rows, :] += g2_ref[0] * jnp.dot(sel_s[2], ybuf[2], preferred_element_type=f32)
            return 0

        lax.fori_loop(1, jnp.maximum(_blocks_needed(cnt_sm, tile, ne), 1), extra, 0)

    t0 = 2 * j
    t1 = t0 + 1
    t2 = jnp.minimum(t0 + 2, n_tiles - 1)

    @pl.when(j == 0)
    def _():
        ybuf[...] = jnp.zeros(ybuf.shape, bf16)
        one_hot(0, 0, slot_ref, aff_ref, 0, 0)
        start_block(0, 0, 0)

    start_block(t1, 0, 1)
    wait_block(0)
    finish(0, 0)
    one_hot(t1, 0, slot_ref, aff_ref, t, 1)
    extra_blocks(t0, 0)

    @pl.when(j < last)
    def _():
        start_block(t2, 0, 0)

    wait_block(1)
    finish(t, 1)
    one_hot(t2, 0, slot_nx, aff_nx, 0, 0)
    extra_blocks(t1, t)


def _combine(base, cnt, slot, aff, x1, gate2, ye, seq):
    n, ne = slot.shape
    d = x1.shape[1]
    t = MOE_TILE
    tiles = n // t
    per_seq = seq // (2 * t)
    nxt = lambda j, *_: (jnp.minimum(2 * j + 2, tiles - 1), 0)
    return pl.pallas_call(
        functools.partial(_combine_kernel, ne=ne),
        grid_spec=pltpu.PrefetchScalarGridSpec(
            num_scalar_prefetch=2,
            grid=(tiles // 2,),
            in_specs=[pl.BlockSpec((2 * t, ne), lambda j, *_: (j, 0)),
                      pl.BlockSpec((2 * t, ne), lambda j, *_: (j, 0)),
                      pl.BlockSpec((t, ne), nxt),
                      pl.BlockSpec((t, ne), nxt),
                      pl.BlockSpec((2 * t, d), lambda j, *_: (j, 0)),
                      pl.BlockSpec((1, 1, d), lambda j, *_: (j // per_seq, 0, 0)),
                      pl.BlockSpec(memory_space=pl.ANY)],
            out_specs=pl.BlockSpec((2 * t, d), lambda j, *_: (j, 0)),
            scratch_shapes=[pltpu.VMEM((3, t, ne * SLOT_LANES), bf16),
                            pltpu.VMEM((3, ne * SLOT_LANES, d), bf16),
                            pltpu.SemaphoreType.DMA((3,))]),
        out_shape=jax.ShapeDtypeStruct((n, d), f32),
        compiler_params=_params("arbitrary"),
        name="combine",
    )(base, cnt, slot, aff, slot, aff, x1, gate2, ye)


def _rope_tables(s):
    t = jnp.arange(s)
    row = (t // GRID_W).astype(f32)
    col = (t % GRID_W).astype(f32)
    axis_dim = HEAD_DIM // 2
    inv = ROPE_THETA ** (-jnp.arange(0, axis_dim, 2, dtype=f32) / axis_dim)
    ang = jnp.concatenate([row[:, None] * inv, col[:, None] * inv], axis=-1)
    cos, sin = jnp.cos(ang), jnp.sin(ang)
    reps = LANES // HEAD_DIM
    cos_t = jnp.tile(jnp.concatenate([cos, cos], axis=-1), (1, reps))
    sin_t = jnp.tile(jnp.concatenate([-sin, sin], axis=-1), (1, reps))
    return cos_t, sin_t


def _head_perm(n_heads):
    half = jnp.concatenate([jnp.arange(0, HEAD_DIM, 2), jnp.arange(1, HEAD_DIM, 2)])
    return (jnp.arange(n_heads)[:, None] * HEAD_DIM + half[None, :]).reshape(-1)


def _block_mean(width):
    idx = jnp.arange(width) // HEAD_DIM
    return jnp.where(idx[:, None] == idx[None, :], 1.0 / HEAD_DIM, 0.0).astype(bf16)


def _prepare(norm_mix, w_in, q_norm, k_norm, conv_w, conv_b, conv_ln_g, conv_ln_b, w_out, norm_ffn,
             w_router, w_gate, w_up, w_down):
    d = w_in.shape[0]
    d_conv = conv_w.shape[-1]
    d_attn = w_out.shape[0] - d_conv
    d_kv = (w_in.shape[1] - d_attn - 2 * d_conv) // 2
    n_heads = d_attn // HEAD_DIM
    n_kv = d_kv // HEAD_DIM
    qp = _head_perm(n_heads)
    kp = _head_perm(n_kv)
    cols = jnp.concatenate([qp, d_attn + kp, jnp.arange(d_attn + d_kv, w_in.shape[1])])
    wr_hi = w_router.astype(bf16)
    wr_lo = (w_router - wr_hi.astype(f32)).astype(bf16)
    score_bound = HEAD_DIM * (HEAD_DIM ** -0.5 * LOG2E) * jnp.max(jnp.abs(q_norm)) * jnp.max(jnp.abs(k_norm))
    return dict(
        d_attn=d_attn, d_kv=d_kv, score_bound=score_bound,
        norm_mix=norm_mix.reshape(1, d),
        w_in=w_in[:, cols].astype(bf16),
        qg=jnp.tile(q_norm[_head_perm(1)], n_heads).reshape(1, d_attn),
        kg=jnp.tile(k_norm[_head_perm(1)], n_kv).reshape(1, d_kv),
        bdq=_block_mean(d_attn), bdk=_block_mean(d_kv),
        conv_w=conv_w.reshape(conv_w.shape[0], d_conv), conv_b=conv_b.reshape(1, d_conv),
        conv_ln_g=conv_ln_g.reshape(1, d_conv), conv_ln_b=conv_ln_b.reshape(1, d_conv),
        w_a=w_out[:d_attn].astype(bf16), w_c=w_out[d_attn:].astype(bf16),
        norm_ffn=norm_ffn.reshape(1, d),
        wr1=jnp.concatenate([wr_hi, wr_lo], axis=1), wr2=wr_hi,
        w_gate=w_gate, w_up=w_up, w_down=w_down,
    )


def _encoder_layer(x, mod, p):
    b, s, d = x.shape
    shift1, scale1, gate1, shift2, scale2, gate2 = [m.reshape(b, 1, d) for m in jnp.split(mod, 6, axis=-1)]
    cos, sin = _rope_tables(s)
    q, k, v, u = _inproj(x, shift1, scale1, p["norm_mix"], p["w_in"], p["qg"], p["kg"], p["bdq"], p["bdk"],
                         cos, sin, p["d_attn"], p["d_kv"])
    attn = lax.cond(p["score_bound"] <= SCORE_BOUND_LOG2,
                    lambda: _attention(q, k, v, True), lambda: _attention(q, k, v, False))
    conv = _conv(u, p["conv_w"], p["conv_b"], p["conv_ln_g"], p["conv_ln_b"])
    x1, h2, aff = _outproj(attn, conv, x, gate1, scale2, shift2, p["norm_ffn"], p["w_a"], p["w_c"],
                           p["wr1"], p["wr2"])

    n = b * s
    ne = aff.shape[-1]
    cap = CAPACITY_FACTOR * n // ne
    aff = aff.reshape(n, ne)
    bits = lax.bitcast_convert_type(aff, jnp.int32).reshape(n * ne // LANES, LANES)
    thr_bits, ntie = _thresh(bits, cap, ne)
    thr = lax.bitcast_convert_type(thr_bits[0, :ne], f32).reshape(ne, 1)
    ntie = ntie[0, :ne].astype(f32).reshape(ne, 1)
    t = MOE_TILE
    tok = jnp.arange(t)
    ut = (tok[:, None] < tok[None, :]).astype(bf16)
    slot_t, cnt, base, used = _plan(aff.T, thr, ntie, ut)
    cnt = cnt.reshape(-1)
    base = base.reshape(-1)
    tiles = n // t
    rows_alloc = -(-(cap + BF16_SUBLANES * tiles + SLOT_BLOCK) // FFN_TILE) * FFN_TILE
    used = used.reshape(-1)
    xe = _gather(base, cnt, used, slot_t, h2.reshape(n, d), rows_alloc)
    ye = _ffn(used, xe, p["w_gate"], p["w_up"], p["w_down"])
    out = _combine(base, cnt, slot_t.T, aff, x1.reshape(n, d), gate2, ye, s)
    return out.reshape(b, s, d)


def kernel(x_prompt, x_sample, c_prompt, c_sample, ada_w, ada_b, norm_mix, w_in, q_norm, k_norm, conv_w, conv_b,
           conv_ln_g, conv_ln_b, w_out, norm_ffn, w_router, w_gate, w_up, w_down):
    y_prompt, y_sample = x_prompt, x_sample
    nb = x_prompt.shape[0]
    for l in range(ada_w.shape[0]):
        p = _prepare(norm_mix[l], w_in[l], q_norm[l], k_norm[l], conv_w[l], conv_b[l], conv_ln_g[l],
                     conv_ln_b[l], w_out[l], norm_ffn[l], w_router[l], w_gate[l], w_up[l], w_down[l])
        mod = _ada(jnp.concatenate([c_prompt, c_sample], axis=0), ada_w[l], ada_b[l])
        y_prompt = _encoder_layer(y_prompt, mod[:nb], p)
        y_sample = _encoder_layer(y_sample, mod[nb:], p)
    return (y_prompt, y_sample)
```
